```python
import jax, jax.numpy as jnp
from jax import lax
import numpy as np

D_MODEL = 1024
BATCH = 16
SEQ = 256
DEPTH = 4
DEC_BATCH = 2
DEC_SEQ = 2048
PAST_LEN = 256

GRID_W = 64
Q_BLOCK = 128
ROPE_THETA = 10000.0
EPS = 1e-6

HEAD_DIM = 64
GQA_HEADS = 8
GQA_KV_HEADS = 2
GQA_GROUP = GQA_HEADS // GQA_KV_HEADS
GQA_Q_W = GQA_HEADS * HEAD_DIM
GQA_KV_W = GQA_KV_HEADS * HEAD_DIM

MLA_HEADS = 8
MLA_Q_RANK = 384
MLA_KV_RANK = 256
MLA_NOPE = 64
MLA_ROPE = 32
MLA_V = 64
MLA_QK = MLA_NOPE + MLA_ROPE
MLA_O_W = MLA_HEADS * MLA_V

CONV_CH = 512
CONV_K = 31
CONV_PAD = CONV_K // 2

N_BRANCH = 3
IN_WIDTHS = (GQA_Q_W, GQA_KV_W, GQA_KV_W, MLA_Q_RANK, MLA_KV_RANK, MLA_ROPE, 2 * CONV_CH, N_BRANCH * D_MODEL)
IN_W = sum(IN_WIDTHS)
IN_SPLITS = tuple(sum(IN_WIDTHS[:i + 1]) for i in range(len(IN_WIDTHS) - 1))

N_EXPERTS = 32
TOP_K = 4
EXPERT_FF = 1024
SWIGLU_LIMIT = 7.0
SWIGLU_ALPHA = 1.702

kernel_name = 'hybrid_diffusion_gqa_mla_conv_moe_step'


def _rms(x, g):
    xf = x.astype(jnp.float32)
    y = xf * lax.rsqrt(jnp.mean(xf * xf, axis=-1, keepdims=True) + EPS)
    return (y * g.astype(jnp.float32)).astype(x.dtype)


def _rot_half(x, pos):
    h = x.shape[-1]
    inv = ROPE_THETA ** (-jnp.arange(0, h, 2, dtype=jnp.float32) / h)
    ang = pos.astype(jnp.float32)[:, None] * inv[None, :]
    shape = (1, pos.shape[0]) + (1,) * (x.ndim - 3) + (h // 2,)
    cos = jnp.cos(ang).reshape(shape)
    sin = jnp.sin(ang).reshape(shape)
    xf = x.astype(jnp.float32)
    x1, x2 = xf[..., :h // 2], xf[..., h // 2:]
    return jnp.concatenate([x1 * cos - x2 * sin, x1 * sin + x2 * cos], axis=-1).astype(x.dtype)


def _axial_rope(x, row, col):
    half = x.shape[-1] // 2
    return jnp.concatenate([_rot_half(x[..., :half], row), _rot_half(x[..., half:], col)], axis=-1)


def _attend(q, k, v):
    b, s, hk, g, dq = q.shape
    scale = dq ** -0.5
    qb = q.reshape(b, s // Q_BLOCK, Q_BLOCK, hk, g, dq).swapaxes(0, 1)

    def one_block(qblk):
        sc = jnp.einsum('bqhgd,bkhd->bhgqk', qblk, k, preferred_element_type=jnp.float32) * scale
        pr = jax.nn.softmax(sc, axis=-1).astype(v.dtype)
        return jnp.einsum('bhgqk,bkhd->bqhgd', pr, v)

    out = lax.map(one_block, qb)
    return out.swapaxes(0, 1).reshape(b, s, hk * g * v.shape[-1])


def _conv_module(u_in, dw, dw_b, ln_g, ln_b, w_pw2):
    a, gt = jnp.split(u_in, 2, axis=-1)
    u = a * jax.nn.sigmoid(gt)
    u = lax.conv_general_dilated(u, dw[:, None, :], window_strides=(1,),
                                 padding=((CONV_PAD, CONV_PAD),),
                                 dimension_numbers=('NWC', 'WIO', 'NWC'),
                                 feature_group_count=CONV_CH) + dw_b
    uf = u.astype(jnp.float32)
    mu = jnp.mean(uf, axis=-1, keepdims=True)
    var = jnp.mean(jnp.square(uf - mu), axis=-1, keepdims=True)
    uf = (uf - mu) * lax.rsqrt(var + EPS) * ln_g.astype(jnp.float32) + ln_b.astype(jnp.float32)
    u = jax.nn.silu(uf).astype(u_in.dtype)
    return u @ w_pw2


def _token_mix(h, p, pos, ctx):
    b, t, _ = h.shape
    q_g, k_g, v_g, cq, ckv, kr, conv_in, gates = jnp.split(h @ p['w_in'], IN_SPLITS, axis=-1)
    q_g = _rms(q_g.reshape(b, t, GQA_KV_HEADS, GQA_GROUP, HEAD_DIM), p['gqa_qn_g'])
    k_g = _rms(k_g.reshape(b, t, GQA_KV_HEADS, HEAD_DIM), p['gqa_kn_g'])
    v_g = v_g.reshape(b, t, GQA_KV_HEADS, HEAD_DIM)
    q_m = _rms((_rms(cq, p['mla_cq_g']) @ p['w_uq']).reshape(b, t, MLA_HEADS, MLA_QK), p['mla_qn_g'])
    ckv = _rms(ckv, p['mla_ckv_g'])
    kr = _rms(kr, p['mla_krn_g'])
    ctx_out = (k_g, v_g, ckv, kr)
    if pos is not None:
        row, col = pos
        q_g = _axial_rope(q_g, row, col)
        k_g = _axial_rope(k_g, row, col)
        q_m = jnp.concatenate([q_m[..., :MLA_NOPE], _axial_rope(q_m[..., MLA_NOPE:], row, col)], axis=-1)
        kr = _axial_rope(kr, row, col)
        ck, cv, cckv, ckr = ctx
        k_g = jnp.concatenate([k_g, ck], axis=1)
        v_g = jnp.concatenate([v_g, cv], axis=1)
        ckv = jnp.concatenate([ckv, cckv], axis=1)
        kr = jnp.concatenate([kr, ckr], axis=1)
    tk = ckv.shape[1]
    kv_m = (ckv @ p['w_ukv']).reshape(b, tk, MLA_HEADS, MLA_NOPE + MLA_V)
    k_m = jnp.concatenate([_rms(kv_m[..., :MLA_NOPE], p['mla_kn_g']),
                           jnp.broadcast_to(kr[:, :, None, :], (b, tk, MLA_HEADS, MLA_ROPE))], axis=-1)
    v_m = kv_m[..., MLA_NOPE:]
    o_gqa = _attend(q_g, k_g, v_g) @ p['w_o_gqa']
    o_mla = _attend(q_m[:, :, :, None, :], k_m, v_m) @ p['w_o_mla']
    o_conv = _conv_module(conv_in, p['conv_dw'], p['conv_dw_b'], p['conv_ln_g'], p['conv_ln_b'], p['w_pw2'])
    g_a, g_b, g_c = jnp.split(jax.nn.sigmoid(gates.astype(jnp.float32)).astype(h.dtype), N_BRANCH, axis=-1)
    merged = g_a * o_gqa + g_b * o_mla + g_c * o_conv
    return merged @ p['w_out'], ctx_out


def _moe(h, p):
    b, t, d = h.shape
    x2 = h.reshape(b * t, d)
    logits = (x2 @ p['w_router'] + p['b_router']).astype(jnp.float32)
    top_v, top_i = lax.top_k(logits, TOP_K)
    top_w = jax.nn.softmax(top_v, axis=-1)
    gate = jnp.sum(jax.nn.one_hot(top_i, N_EXPERTS, dtype=jnp.float32) * top_w[..., None], axis=1).astype(h.dtype)
    out = jnp.zeros_like(x2)
    for e in range(N_EXPERTS):
        gl, up = jnp.split(x2 @ p['w_gu'][e] + p['b_gu'][e], 2, axis=-1)
        gl = jnp.minimum(gl, SWIGLU_LIMIT)
        up = jnp.clip(up, -SWIGLU_LIMIT, SWIGLU_LIMIT)
        y = ((up + 1.0) * (gl * jax.nn.sigmoid(SWIGLU_ALPHA * gl))) @ p['w_dn'][e] + p['b_dn'][e]
        out = out + gate[:, e:e + 1] * y
    return out.reshape(b, t, d)


def _layer(x, mod, p, pos, ctx):
    sh1, sc1, g1, sh2, sc2, g2 = jnp.split(mod, 6, axis=-1)
    h = _rms(x, p['norm1_g']) * (1.0 + sc1) + sh1
    o, ctx_out = _token_mix(h, p, pos, ctx)
    x = x + g1 * o
    h = _rms(x, p['norm2_g']) * (1.0 + sc2) + sh2
    x = x + g2 * _moe(h, p)
    return x, ctx_out


def setup_inputs(seed: int = 0) -> dict:
    key = jax.random.key(seed)
    ks = iter(jax.random.split(key, 48))

    def nrm(shape, scale):
        return scale * jax.random.normal(next(ks), shape, jnp.float32)

    def gain(shape):
        return 1.0 + nrm(shape, 0.05)

    L, D, E, F = DEPTH, D_MODEL, N_EXPERTS, EXPERT_FF
    return {
        'x_prompt': nrm((BATCH, SEQ, D), 1.0),
        'x_sample': nrm((DEC_BATCH, DEC_SEQ, D), 1.0),
        'cache_gqa_k': nrm((DEC_BATCH, L, PAST_LEN, GQA_KV_HEADS, HEAD_DIM), 1.0),
        'cache_gqa_v': nrm((DEC_BATCH, L, PAST_LEN, GQA_KV_HEADS, HEAD_DIM), 1.0),
        'cache_mla_ckv': nrm((DEC_BATCH, L, PAST_LEN, MLA_KV_RANK), 1.0),
        'cache_mla_krope': nrm((DEC_BATCH, L, PAST_LEN, MLA_ROPE), 1.0),
        'c': nrm((DEC_BATCH, D), 1.0),
        'c_ctx': nrm((D,), 1.0),
        'w_ada': nrm((L, D, 6 * D), 0.5 * D ** -0.5),
        'b_ada': nrm((L, 6 * D), 0.02),
        'norm1_g': gain((L, D)),
        'w_in': nrm((L, D, IN_W), D ** -0.5),
        'gqa_qn_g': gain((L, HEAD_DIM)),
        'gqa_kn_g': gain((L, HEAD_DIM)),
        'w_o_gqa': nrm((L, GQA_Q_W, D), GQA_Q_W ** -0.5),
        'mla_cq_g': gain((L, MLA_Q_RANK)),
        'w_uq': nrm((L, MLA_Q_RANK, MLA_HEADS * MLA_QK), MLA_Q_RANK ** -0.5),
        'mla_ckv_g': gain((L, MLA_KV_RANK)),
        'w_ukv': nrm((L, MLA_KV_RANK, MLA_HEADS * (MLA_NOPE + MLA_V)), MLA_KV_RANK ** -0.5),
        'mla_qn_g': gain((L, MLA_QK)),
        'mla_kn_g': gain((L, MLA_NOPE)),
        'mla_krn_g': gain((L, MLA_ROPE)),
        'w_o_mla': nrm((L, MLA_O_W, D), MLA_O_W ** -0.5),
        'conv_dw': nrm((L, CONV_K, CONV_CH), CONV_K ** -0.5),
        'conv_dw_b': nrm((L, CONV_CH), 0.02),
        'conv_ln_g': gain((L, CONV_CH)),
        'conv_ln_b': nrm((L, CONV_CH), 0.02),
        'w_pw2': nrm((L, CONV_CH, D), CONV_CH ** -0.5),
        'w_out': nrm((L, D, D), D ** -0.5),
        'norm2_g': gain((L, D)),
        'w_router': nrm((L, D, E), D ** -0.5),
        'b_router': nrm((L, E), 0.01),
        'w_gu': nrm((L, E, D, 2 * F), D ** -0.5),
        'b_gu': nrm((L, E, 2 * F), 0.02),
        'w_dn': nrm((L, E, F, D), F ** -0.5),
        'b_dn': nrm((L, E, D), 0.02),
    }


def reference(x_prompt, x_sample, cache_gqa_k, cache_gqa_v, cache_mla_ckv, cache_mla_krope, c, c_ctx,
              w_ada, b_ada, norm1_g, w_in, gqa_qn_g, gqa_kn_g, w_o_gqa, mla_cq_g, w_uq, mla_ckv_g, w_ukv,
              mla_qn_g, mla_kn_g, mla_krn_g, w_o_mla, conv_dw, conv_dw_b, conv_ln_g, conv_ln_b, w_pw2,
              w_out, norm2_g, w_router, b_router, w_gu, b_gu, w_dn, b_dn):
    n_lat = x_sample.shape[1]
    rows = n_lat // GRID_W
    row = jnp.repeat(jnp.arange(rows, dtype=jnp.int32), GRID_W)
    col = jnp.tile(jnp.arange(GRID_W, dtype=jnp.int32), rows)
    pos = (row, col)

    xp, xs = x_prompt, x_sample
    ks, vs, ckvs, krs = [], [], [], []
    for l in range(DEPTH):
        p = {
            'norm1_g': norm1_g[l], 'w_in': w_in[l], 'gqa_qn_g': gqa_qn_g[l], 'gqa_kn_g': gqa_kn_g[l],
            'w_o_gqa': w_o_gqa[l], 'mla_cq_g': mla_cq_g[l], 'w_uq': w_uq[l], 'mla_ckv_g': mla_ckv_g[l],
            'w_ukv': w_ukv[l], 'mla_qn_g': mla_qn_g[l], 'mla_kn_g': mla_kn_g[l], 'mla_krn_g': mla_krn_g[l],
            'w_o_mla': w_o_mla[l], 'conv_dw': conv_dw[l], 'conv_dw_b': conv_dw_b[l],
            'conv_ln_g': conv_ln_g[l], 'conv_ln_b': conv_ln_b[l], 'w_pw2': w_pw2[l], 'w_out': w_out[l],
            'norm2_g': norm2_g[l], 'w_router': w_router[l], 'b_router': b_router[l],
            'w_gu': w_gu[l], 'b_gu': b_gu[l], 'w_dn': w_dn[l], 'b_dn': b_dn[l],
        }
        mod_p = (jax.nn.silu(c_ctx)[None, :] @ w_ada[l] + b_ada[l])[:, None, :]
        mod_s = (jax.nn.silu(c) @ w_ada[l] + b_ada[l])[:, None, :]
        xp, (k_c, v_c, ckv_c, kr_c) = _layer(xp, mod_p, p, None, None)
        ks.append(k_c)
        vs.append(v_c)
        ckvs.append(ckv_c)
        krs.append(kr_c)
        ctx = (cache_gqa_k[:, l], cache_gqa_v[:, l], cache_mla_ckv[:, l], cache_mla_krope[:, l])
        xs, _ = _layer(xs, mod_s, p, pos, ctx)

    new_gqa_k = jnp.stack(ks, axis=1)
    new_gqa_v = jnp.stack(vs, axis=1)
    new_mla_ckv = jnp.stack(ckvs, axis=1)
    new_mla_krope = jnp.stack(krs, axis=1)
    return (xp, xs, new_gqa_k, new_gqa_v, new_mla_ckv, new_mla_krope)
```

```python
import functools

import numpy as np
import jax
import jax.numpy as jnp
from jax import lax
from jax.experimental import pallas as pl
from jax.experimental.pallas import tpu as pltpu

F32 = jnp.float32
BF16 = jnp.bfloat16
I32 = jnp.int32

D = 1024
DEPTH = 4
N_CTX_B, S_CTX = 16, 256
N_LAT_B, S_LAT = 2, 2048
PAST = 256
GRID_W = 64
ROPE_THETA = 10000.0
EPS = 1e-6
HD = 64
NH = 8
GQA_KV = 2
GQA_GROUP = NH // GQA_KV
MLA_Q_RANK = 384
MLA_KV_RANK = 256
MLA_ROPE = 32
MLA_QK = HD + MLA_ROPE
CONV_CH = 512
CONV_K = 31
CONV_PAD = CONV_K // 2
N_EXP = 32
TOP_K = 4
FF = 1024
SWIGLU_LIMIT = 7.0
SWIGLU_ALPHA = 1.702

T_CTX = N_CTX_B * S_CTX
T_LAT = N_LAT_B * S_LAT
T = T_CTX + T_LAT

LANES = 128
TM = 256
N_TILES = T // TM
CTX_TILES = T_CTX // TM
LAT_TILES_PER_B = S_LAT // TM
WA = 2560
GATES_W = 3 * D
CH = 64
N_PASS_MAX = TM // CH
SLOTS = N_EXP * CH
ET = 256
XS_W = D + LANES
N_ET_MAX = (T * TOP_K + 7 * N_TILES * N_EXP) // ET + N_EXP
N_SLOTS = N_ET_MAX * ET + N_EXP * ET
VMEM_LIMIT = 56 * 1024 * 1024


def _dot(a, b):
    return jnp.dot(a, b, preferred_element_type=F32)


def _dot_nt(a, b):
    return lax.dot_general(a, b, (((1,), (1,)), ((), ())), preferred_element_type=F32)


def _split3(x):
    hi = x.astype(BF16)
    r = x - hi.astype(F32)
    mid = r.astype(BF16)
    lo = (r - mid.astype(F32)).astype(BF16)
    return hi, mid, lo


def _cparams(sem, vmem=VMEM_LIMIT):
    return pltpu.CompilerParams(dimension_semantics=sem, vmem_limit_bytes=vmem)


def _grp(i):
    return jnp.where(i < CTX_TILES, 0, 1 + (i - CTX_TILES) // LAT_TILES_PER_B)


def _pos_blk(i):
    return jnp.where(i < CTX_TILES, 0, 1 + (i - CTX_TILES) % LAT_TILES_PER_B)


def _ada_kernel(c_ref, w_ref, b_ref, o_ref):
    c = c_ref[...]
    s = c * jax.nn.sigmoid(c)
    w = w_ref[...]
    s_hi, s_lo, _ = _split3(s)
    w_hi, w_lo, _ = _split3(w)
    o_ref[...] = _dot(s_hi, w_hi) + _dot(s_hi, w_lo) + _dot(s_lo, w_hi) + b_ref[...]


def _ada_call(cvec, w_ada, b_ada):
    tn = 1536
    return pl.pallas_call(
        _ada_kernel,
        grid=(DEPTH, 6 * D // tn),
        in_specs=[
            pl.BlockSpec((8, D), lambda l, j: (0, 0)),
            pl.BlockSpec((None, D, tn), lambda l, j: (l, 0, j)),
            pl.BlockSpec((None, 1, tn), lambda l, j: (l, 0, j)),
        ],
        out_specs=pl.BlockSpec((None, 8, tn), lambda l, j: (l, 0, j)),
        out_shape=jax.ShapeDtypeStruct((DEPTH, 8, 6 * D), F32),
        compiler_params=_cparams(("arbitrary", "arbitrary")),
        name="ada_mod",
    )(cvec, w_ada, b_ada.reshape(DEPTH, 1, 6 * D))


def _norm_mod(x, g, scale, shift):
    ms = jnp.mean(x * x, axis=-1, keepdims=True)
    return (x * lax.rsqrt(ms + EPS) * g) * (1.0 + scale) + shift


def _group_rms(x, ss, n, g):
    return x * lax.rsqrt(ss * (1.0 / n) + EPS) * g


def _partner(x, dist):
    n = x.shape[-1]
    lane = lax.broadcasted_iota(I32, x.shape, 1)
    r1 = pltpu.roll(x, dist, 1)
    r2 = pltpu.roll(x, n - dist, 1)
    r1_is_minus = pltpu.roll(lane, dist, 1) == ((lane + (n - dist)) & (n - 1))
    plus = jnp.where(r1_is_minus, r2, r1)
    minus = jnp.where(r1_is_minus, r1, r2)
    return jnp.where((lane & (2 * dist - 1)) < dist, plus, minus)


def _rope(x, cos, sin_signed, dist):
    return x * cos + _partner(x, dist) * sin_signed


def _tile_lanes(x, reps):
    return jnp.concatenate([x] * reps, axis=1)


def _inproj_kernel(x_ref, mod_ref, g1_ref, wa_ref, wuq_ref, wukv_ref,
                   qng_ref, kng_ref, cqg_ref, ckvg_ref, krg_ref, mqn_ref, mqr_ref, mkn_ref,
                   c64_ref, s64_ref, c32_ref, s32_ref,
                   b64_ref, brn_ref, bnr_ref, brr_ref,
                   qg_ref, kg_ref, vg_ref, qn_ref, qr_ref, ckv_ref, kr_ref, kn_ref, vm_ref, u_ref):
    mod = mod_ref[...]
    h = _norm_mod(x_ref[...], g1_ref[...], mod[:, D:2 * D], mod[:, 0:D])
    p = _dot(h.astype(BF16), wa_ref[...])
    b64 = b64_ref[...]
    c64 = c64_ref[...]
    s64 = s64_ref[...]
    c32 = c32_ref[...]
    s32 = s32_ref[...]

    q = p[:, 0:512]
    q = _group_rms(q, _dot((q * q).astype(BF16), b64), HD, qng_ref[...])
    q = _rope(q, _tile_lanes(c64, 4), _tile_lanes(s64, 4), 16)
    qg_ref[...] = (q * (HD ** -0.5)).astype(BF16)
    k = p[:, 512:640]
    k = _group_rms(k, _dot((k * k).astype(BF16), b64[0:128, 0:128]), HD, kng_ref[...])
    kg_ref[...] = _rope(k, c64, s64, 16)
    vg_ref[...] = p[:, 640:768]

    cq = p[:, 768:1152]
    cq = cq * lax.rsqrt(jnp.mean(cq * cq, axis=-1, keepdims=True) + EPS) * cqg_ref[...]
    qm = _dot(cq.astype(BF16), wuq_ref[...])
    qmn = qm[:, 0:512]
    qmr = qm[:, 512:768]
    sqn = (qmn * qmn).astype(BF16)
    sqr = (qmr * qmr).astype(BF16)
    ss_n = _dot(sqn, b64) + _dot(sqr, brn_ref[...])
    ss_r = _dot(sqn, bnr_ref[...]) + _dot(sqr, brr_ref[...])
    scale = MLA_QK ** -0.5
    qn_ref[...] = (_group_rms(qmn, ss_n, MLA_QK, mqn_ref[...]) * scale).astype(BF16)
    qmr = _group_rms(qmr, ss_r, MLA_QK, mqr_ref[...])
    qr_ref[...] = (_rope(qmr, _tile_lanes(c32, 2), _tile_lanes(s32, 2), 8) * scale).astype(BF16)

    ckv = p[:, 1152:1408]
    ckv = ckv * lax.rsqrt(jnp.mean(ckv * ckv, axis=-1, keepdims=True) + EPS) * ckvg_ref[...]
    ckv_ref[...] = ckv
    kv = _dot(ckv.astype(BF16), wukv_ref[...])
    kn = kv[:, 0:512]
    kn_ref[...] = _group_rms(kn, _dot((kn * kn).astype(BF16), b64), HD, mkn_ref[...]).astype(BF16)
    vm_ref[...] = kv[:, 512:1024].astype(BF16)

    kr = p[:, 1408:1536]
    kr = kr * lax.rsqrt(jnp.sum(kr * kr, axis=-1, keepdims=True) * (1.0 / MLA_ROPE) + EPS) * krg_ref[...]
    kr_ref[...] = _rope(kr, c32, s32, 8)

    u_ref[...] = p[:, 1536:2048] * jax.nn.sigmoid(p[:, 2048:2560])


def _inproj_call(l, x, mods, lw, consts):
    def full(a):
        return pl.BlockSpec(a.shape, lambda i: (0,) * a.ndim)

    def layer(a):
        return pl.BlockSpec((None,) + a.shape[1:], lambda i: (l,) + (0,) * (a.ndim - 1))

    def rows(w):
        return pl.BlockSpec((TM, w), lambda i: (i, 0))

    def tab():
        return pl.BlockSpec((TM, LANES), lambda i: (_pos_blk(i), 0))

    layer_ws = [lw["norm1_g"], lw["wa"], lw["w_uq"], lw["w_ukv"], lw["gqa_qn_g"], lw["gqa_kn_g"],
                lw["mla_cq_g"], lw["mla_ckv_g"], lw["mla_krn_g"], lw["mla_qn_n"], lw["mla_qn_r"],
                lw["mla_kn_g"]]
    tabs = [consts["c64"], consts["s64"], consts["c32"], consts["s32"]]
    bms = [consts["b64"], consts["brn"], consts["bnr"], consts["brr"]]
    out_w = [(512, BF16), (128, F32), (128, F32), (512, BF16), (256, BF16), (256, F32), (128, F32),
             (512, BF16), (512, BF16), (512, F32)]
    return pl.pallas_call(
        _inproj_kernel,
        grid=(N_TILES,),
        in_specs=[rows(D), pl.BlockSpec((None, 1, 6 * D), lambda i: (l * 8 + _grp(i), 0, 0))]
        + [layer(a) for a in layer_ws] + [tab() for _ in tabs] + [full(a) for a in bms],
        out_specs=[rows(w) for w, _ in out_w],
        out_shape=[jax.ShapeDtypeStruct((T, w), dt) for w, dt in out_w],
        compiler_params=_cparams(("arbitrary",)),
        name="inproj",
    )(x, mods, *layer_ws, *tabs, *bms)


def _cache_kv_kernel(ckv_ref, wukv_ref, mkn_ref, b64_ref, kn_ref, vm_ref):
    kv = _dot(ckv_ref[...].astype(BF16), wukv_ref[...])
    kn = kv[:, 0:512]
    kn_ref[...] = _group_rms(kn, _dot((kn * kn).astype(BF16), b64_ref[...]), HD, mkn_ref[...]).astype(BF16)
    vm_ref[...] = kv[:, 512:1024].astype(BF16)


def _cache_kv_call(cache_ckv, w_ukv, mla_kn_g, b64):
    out = pl.BlockSpec((None, None, PAST, 512), lambda b, l: (b, l, 0, 0))
    return pl.pallas_call(
        _cache_kv_kernel,
        grid=(N_LAT_B, DEPTH),
        in_specs=[
            pl.BlockSpec((None, None, PAST, MLA_KV_RANK), lambda b, l: (b, l, 0, 0)),
            pl.BlockSpec((None, MLA_KV_RANK, 1024), lambda b, l: (l, 0, 0)),
            pl.BlockSpec((None, 1, 512), lambda b, l: (l, 0, 0)),
            pl.BlockSpec((512, 512), lambda b, l: (0, 0)),
        ],
        out_specs=[out, out],
        out_shape=[jax.ShapeDtypeStruct((N_LAT_B, DEPTH, PAST, 512), BF16)] * 2,
        compiler_params=_cparams(("arbitrary", "arbitrary")),
        name="cache_kv",
    )(cache_ckv, w_ukv, mla_kn_g, b64)


def _attn_kernel(*refs, mla, has_cache):
    it = iter(refs)
    q_ref = next(it)
    qr_ref = next(it) if mla else None
    ks_ref = next(it)
    vs_ref = next(it)
    krs_ref = next(it) if mla else None
    kc_ref = vc_ref = krc_ref = None
    if has_cache:
        kc_ref = next(it)
        vc_ref = next(it)
        krc_ref = next(it) if mla else None
    o_ref = next(it)

    if mla:
        krs = krs_ref[:, 0:MLA_ROPE].astype(BF16)
        krc = krc_ref[...].astype(BF16) if has_cache else None
    outs = []
    for h in range(NH):
        kvh = h if mla else h // GQA_GROUP
        sl = slice(HD * kvh, HD * kvh + HD)
        qh = q_ref[:, HD * h:HD * h + HD]
        s = _dot_nt(qh, ks_ref[:, sl].astype(BF16))
        if mla:
            qrh = qr_ref[:, MLA_ROPE * h:MLA_ROPE * h + MLA_ROPE]
            s = s + _dot_nt(qrh, krs)
        m = jnp.max(s, axis=-1, keepdims=True)
        if has_cache:
            sc = _dot_nt(qh, kc_ref[:, sl].astype(BF16))
            if mla:
                sc = sc + _dot_nt(qrh, krc)
            m = jnp.maximum(m, jnp.max(sc, axis=-1, keepdims=True))
        p = jnp.exp(s - m)
        den = jnp.sum(p, axis=-1, keepdims=True)
        o = _dot(p.astype(BF16), vs_ref[:, sl].astype(BF16))
        if has_cache:
            pc = jnp.exp(sc - m)
            den = den + jnp.sum(pc, axis=-1, keepdims=True)
            o = o + _dot(pc.astype(BF16), vc_ref[:, sl].astype(BF16))
        outs.append((o / den).astype(BF16))
    o_ref[...] = jnp.concatenate(outs, axis=1)


def _attn_call(l, latent, mla, q, qr, ks, vs, krs, kc, vc, krc):
    kw = ks.shape[1]
    if latent:
        nb, nq, n_self, q0, k0 = N_LAT_B, S_LAT // TM, S_LAT, CTX_TILES, T_CTX // S_LAT
    else:
        nb, nq, n_self, q0, k0 = N_CTX_B, 1, S_CTX, 0, 0

    def qspec(w):
        return pl.BlockSpec((TM, w), lambda b, i: (q0 + b * nq + i, 0))

    def kspec(w):
        return pl.BlockSpec((n_self, w), lambda b, i: (k0 + b, 0))

    def cspec(w):
        return pl.BlockSpec((None, None, PAST, w), lambda b, i: (b, l, 0, 0))

    args, specs = [q], [qspec(512)]
    if mla:
        args.append(qr)
        specs.append(qspec(256))
    args += [ks, vs]
    specs += [kspec(kw), kspec(kw)]
    if mla:
        args.append(krs)
        specs.append(kspec(LANES))
    if latent:
        args += [kc, vc]
        specs += [cspec(kw), cspec(kw)]
        if mla:
            args.append(krc)
            specs.append(cspec(MLA_ROPE))
    return pl.pallas_call(
        functools.partial(_attn_kernel, mla=mla, has_cache=latent),
        grid=(nb, nq),
        in_specs=specs,
        out_specs=pl.BlockSpec((TM, 512), lambda b, i: (b * nq + i, 0)),
        out_shape=jax.ShapeDtypeStruct((nb * nq * TM, 512), BF16),
        compiler_params=_cparams(("arbitrary", "arbitrary")),
        name=("mla" if mla else "gqa") + ("_lat" if latent else "_ctx"),
    )(*args)


def _conv_kernel(up_ref, uc_ref, un_ref, dw_ref, dwb_ref, lng_ref, lnb_ref, o_ref, win_ref):
    i = pl.program_id(0)
    in_seq = (i - CTX_TILES) % LAT_TILES_PER_B
    has_prev = jnp.logical_and(i >= CTX_TILES, in_seq != 0)
    has_next = jnp.logical_and(i >= CTX_TILES, in_seq != LAT_TILES_PER_B - 1)
    win_ref[0:16, :] = jnp.where(has_prev, up_ref[TM - 16:TM, :], 0.0)
    win_ref[16:16 + TM, :] = uc_ref[...]
    win_ref[16 + TM:32 + TM, :] = jnp.where(has_next, un_ref[0:16, :], 0.0)
    acc = jnp.zeros((TM, CONV_CH), F32) + dwb_ref[...]
    for j in range(CONV_K):
        acc = acc + win_ref[j + 1:j + 1 + TM, :] * dw_ref[j:j + 1, :]
    mu = jnp.mean(acc, axis=-1, keepdims=True)
    cen = acc - mu
    var = jnp.mean(cen * cen, axis=-1, keepdims=True)
    y = cen * lax.rsqrt(var + EPS) * lng_ref[...] + lnb_ref[...]
    o_ref[...] = (y * jax.nn.sigmoid(y)).astype(BF16)


def _conv_call(l, u, lw):
    def layer(a):
        return pl.BlockSpec((None,) + a.shape[1:], lambda i: (l,) + (0,) * (a.ndim - 1))

    ws = [lw["conv_dw"], lw["conv_dw_b"], lw["conv_ln_g"], lw["conv_ln_b"]]
    return pl.pallas_call(
        _conv_kernel,
        grid=(N_TILES,),
        in_specs=[
            pl.BlockSpec((TM, CONV_CH), lambda i: (jnp.maximum(i - 1, 0), 0)),
            pl.BlockSpec((TM, CONV_CH), lambda i: (i, 0)),
            pl.BlockSpec((TM, CONV_CH), lambda i: (jnp.minimum(i + 1, N_TILES - 1), 0)),
        ] + [layer(a) for a in ws],
        out_specs=pl.BlockSpec((TM, CONV_CH), lambda i: (i, 0)),
        out_shape=jax.ShapeDtypeStruct((T, CONV_CH), BF16),
        scratch_shapes=[pltpu.VMEM((TM + 32, CONV_CH), F32)],
        compiler_params=_cparams(("arbitrary",)),
        name="conv",
    )(u, u, u, *ws)


def _merge_kernel(x_ref, mod_ref, g1_ref, wg_ref, agc_ref, agl_ref, amc_ref, aml_ref, cv_ref,
                  wog_ref, wom_ref, wpw_ref, wout_ref, g2_ref, wr_hi_ref, wr_lo_ref, br_ref,
                  x1_ref, h2_ref, gt_ref, cnt_ref):
    i = pl.program_id(0)
    is_ctx = i < CTX_TILES
    x = x_ref[...]
    mod = mod_ref[...]
    h = _norm_mod(x, g1_ref[...], mod[:, D:2 * D], mod[:, 0:D]).astype(BF16)
    gates = jax.nn.sigmoid(_dot(h, wg_ref[...]))
    ag = jnp.where(is_ctx, agc_ref[...], agl_ref[...])
    am = jnp.where(is_ctx, amc_ref[...], aml_ref[...])
    merged = (gates[:, 0:D] * _dot(ag, wog_ref[...])
              + gates[:, D:2 * D] * _dot(am, wom_ref[...])
              + gates[:, 2 * D:3 * D] * _dot(cv_ref[...], wpw_ref[...]))
    x1 = x + mod[:, 2 * D:3 * D] * _dot(merged.astype(BF16), wout_ref[...])
    x1_ref[...] = x1
    h2 = _norm_mod(x1, g2_ref[...], mod[:, 4 * D:5 * D], mod[:, 3 * D:4 * D])
    h2_hi = h2.astype(BF16)
    h2_ref[...] = h2_hi
    h2_lo = (h2 - h2_hi.astype(F32)).astype(BF16)
    wr_hi = wr_hi_ref[...]
    logit = _dot_nt(wr_hi, h2_hi) + _dot_nt(wr_hi, h2_lo) + _dot_nt(wr_lo_ref[...], h2_hi) + br_ref[...]
    row = lax.broadcasted_iota(I32, logit.shape, 0).astype(F32)
    sels, vals = [], []
    for _ in range(TOP_K):
        m = jnp.max(logit, axis=0, keepdims=True)
        idx = jnp.min(jnp.where(logit == m, row, float(N_EXP)), axis=0, keepdims=True)
        sel = row == idx
        sels.append(sel)
        vals.append(m)
        logit = jnp.where(sel, -jnp.inf, logit)
    es = [jnp.exp(v - vals[0]) for v in vals]
    den = es[0] + es[1] + es[2] + es[3]
    gate = jnp.zeros(logit.shape, F32)
    for sel, e in zip(sels, es):
        gate = jnp.where(sel, e / den, gate)
    gt_ref[...] = gate
    n = jnp.sum((gate > 0.0).astype(F32), axis=1, keepdims=True)
    cnt_ref[...] = jnp.broadcast_to(n, (N_EXP, LANES)).astype(I32)


def _merge_call(l, x, mods, lw, ag_c, ag_l, am_c, am_l, cv):
    def layer(a):
        return pl.BlockSpec((None,) + a.shape[1:], lambda i: (l,) + (0,) * (a.ndim - 1))

    def rows(w):
        return pl.BlockSpec((TM, w), lambda i: (i, 0))

    ctx_rows = pl.BlockSpec((TM, 512), lambda i: (jnp.minimum(i, CTX_TILES - 1), 0))
    lat_rows = pl.BlockSpec((TM, 512), lambda i: (jnp.maximum(i - CTX_TILES, 0), 0))
    ws1 = [lw["norm1_g"], lw["wg"]]
    ws2 = [lw["w_o_gqa"], lw["w_o_mla"], lw["w_pw2"], lw["w_out"], lw["norm2_g"],
           lw["wr_hi"], lw["wr_lo"], lw["b_router"]]
    return pl.pallas_call(
        _merge_kernel,
        grid=(N_TILES,),
        in_specs=[rows(D), pl.BlockSpec((None, 1, 6 * D), lambda i: (l * 8 + _grp(i), 0, 0))]
        + [layer(a) for a in ws1] + [ctx_rows, lat_rows, ctx_rows, lat_rows, rows(512)]
        + [layer(a) for a in ws2],
        out_specs=[rows(D), rows(D), pl.BlockSpec((N_EXP, TM), lambda i: (0, i)),
                   pl.BlockSpec((None, N_EXP, LANES), lambda i: (i, 0, 0))],
        out_shape=[jax.ShapeDtypeStruct((T, D), F32), jax.ShapeDtypeStruct((T, D), BF16),
                   jax.ShapeDtypeStruct((N_EXP, T), F32),
                   jax.ShapeDtypeStruct((N_TILES, N_EXP, LANES), I32)],
        compiler_params=_cparams(("arbitrary",)),
        name="merge",
    )(x, mods, *ws1, ag_c, ag_l, am_c, am_l, cv, *ws2)


def _moe_tables(cnt):
    n8 = (cnt + 7) // 8 * 8
    tot = jnp.sum(n8, axis=0)
    ntile = (tot + ET - 1) // ET
    seg = ntile * ET + ET
    off = jnp.cumsum(seg) - seg
    start = off[None, :] + jnp.cumsum(n8, axis=0) - n8
    npass = jnp.maximum(1, (jnp.max(cnt, axis=1) + CH - 1) // CH)
    proc_end = off + ntile * ET
    tcum = jnp.cumsum(ntile)
    n_et = tcum[-1]
    j = jnp.minimum(jnp.arange(N_ET_MAX, dtype=I32), n_et - 1)
    te = jnp.searchsorted(tcum, j, side="right").astype(I32)
    tr = off[te] // ET + (j - (tcum - ntile)[te])
    want = start[None] + CH * jnp.arange(N_PASS_MAX, dtype=I32)[:, None, None]
    rd = jnp.where(tot[None, None, :] > 0, jnp.minimum(want, (proc_end - CH)[None, None, :]), tr[0] * ET)
    shift = (want - rd).astype(F32)
    return dict(start=start.reshape(-1).astype(I32), npass=npass.astype(I32),
                flush=(off + tot).astype(I32), te=te, tr=tr.astype(I32),
                n_et=n_et.reshape(1).astype(I32), rd=rd.reshape(-1).astype(I32),
                shift=jnp.broadcast_to(jnp.transpose(shift, (1, 0, 2))[..., None],
                                       (N_TILES, N_PASS_MAX, N_EXP, LANES)))


def _rank_in_tile(gate_t):
    sel = gate_t > 0.0
    r = lax.broadcasted_iota(I32, (TM, TM), 0)
    c = lax.broadcasted_iota(I32, (TM, TM), 1)
    upper = (r < c).astype(BF16)
    rank = _dot(sel.astype(BF16), upper)
    return jnp.where(sel, rank, -1e4)


def _dispatch_kernel(start_ref, npass_ref, flush_ref, gt_ref, h2_ref, xs_ref,
                     buf_ref, zero_ref, sem_ref, zsem_ref, ctr_ref):
    i = pl.program_id(0)

    def chunk_copy(slot, e, dst):
        return pltpu.make_async_copy(buf_ref.at[slot, pl.ds(e * CH, CH)],
                                     xs_ref.at[pl.ds(pl.multiple_of(dst, 8), CH)], sem_ref.at[slot])

    def wait_chunks(slot):
        for e in range(N_EXP):
            chunk_copy(slot, e, start_ref[e]).wait()

    @pl.when(i == 0)
    def _():
        ctr_ref[0] = 0
        zero_ref[...] = jnp.zeros(zero_ref.shape, F32)
        copies = [pltpu.make_async_copy(
            zero_ref, xs_ref.at[pl.ds(pl.multiple_of(flush_ref[e], 8), ET)], zsem_ref.at[0])
            for e in range(N_EXP)]
        for cp in copies:
            cp.start()
        for cp in copies:
            cp.wait()

    gate_t = gt_ref[...]
    rank = _rank_in_tile(gate_t)
    g_hi, g_mid, g_lo = _split3(gate_t)
    h2 = h2_ref[...]
    sub = lax.broadcasted_iota(I32, (ET, N_EXP), 0)
    col = lax.broadcasted_iota(I32, (ET, N_EXP), 1)
    jrow = (lax.broadcasted_iota(I32, (ET, TM), 0) & (CH - 1)).astype(F32)

    def one_pass(p, carry):
        n_done = ctr_ref[0]
        slot = n_done & 1
        rk = (rank - (p * CH).astype(F32)).astype(BF16)
        for c in range(SLOTS // ET):
            expand = (((sub + c * ET) >> 6) == col).astype(BF16)
            hit = _dot(expand, rk) == jrow
            gexp = _dot(expand, g_hi) + _dot(expand, g_mid) + _dot(expand, g_lo)
            w = jnp.sum(jnp.where(hit, gexp, 0.0), axis=1, keepdims=True)
            buf_ref[slot, c * ET:(c + 1) * ET, 0:D] = _dot(hit.astype(BF16), h2)
            buf_ref[slot, c * ET:(c + 1) * ET, D:XS_W] = jnp.broadcast_to(w, (ET, LANES))

        @pl.when(n_done > 0)
        def _():
            wait_chunks(1 - slot)

        for e in range(N_EXP):
            chunk_copy(slot, e, start_ref[i * N_EXP + e] + p * CH).start()
        ctr_ref[0] = n_done + 1
        return carry

    lax.fori_loop(0, npass_ref[i], one_pass, 0)

    @pl.when(i == pl.num_programs(0) - 1)
    def _():
        wait_chunks((ctr_ref[0] - 1) & 1)


def _dispatch_call(tb, gate_t, h2):
    return pl.pallas_call(
        _dispatch_kernel,
        grid_spec=pltpu.PrefetchScalarGridSpec(
            num_scalar_prefetch=3,
            grid=(N_TILES,),
            in_specs=[pl.BlockSpec((N_EXP, TM), lambda i, *_: (0, i)),
                      pl.BlockSpec((TM, D), lambda i, *_: (i, 0))],
            out_specs=pl.BlockSpec(memory_space=pl.ANY),
            scratch_shapes=[pltpu.VMEM((2, SLOTS, XS_W), F32), pltpu.VMEM((ET, XS_W), F32),
                            pltpu.SemaphoreType.DMA((2,)), pltpu.SemaphoreType.DMA((1,)),
                            pltpu.SMEM((1,), I32)]),
        out_shape=jax.ShapeDtypeStruct((N_SLOTS, XS_W), F32),
        compiler_params=_cparams(("arbitrary",)),
        name="moe_dispatch",
    )(tb["start"], tb["npass"], tb["flush"], gate_t, h2)


def _expert_kernel(te_ref, tr_ref, net_ref, xs_ref, wgu_ref, bgu_ref, wdn_ref, bdn_ref, ys_ref,
                   wgu_bf_ref, wdn_bf_ref):
    j = pl.program_id(0)
    live = j < net_ref[0]
    new_expert = jnp.logical_or(j == 0, te_ref[j] != te_ref[jnp.maximum(j - 1, 0)])

    @pl.when(jnp.logical_and(live, new_expert))
    def _():
        for r in range(0, D, 128):
            wgu_bf_ref[r:r + 128, :] = wgu_ref[r:r + 128, :].astype(BF16)
        for r in range(0, FF, 128):
            wdn_bf_ref[r:r + 128, :] = wdn_ref[r:r + 128, :].astype(BF16)

    @pl.when(live)
    def _():
        x = xs_ref[:, 0:D].astype(BF16)
        w = xs_ref[:, D:D + 1]
        gu = _dot(x, wgu_bf_ref[...]) + bgu_ref[...]
        gl = jnp.minimum(gu[:, 0:FF], SWIGLU_LIMIT)
        up = jnp.clip(gu[:, FF:2 * FF], -SWIGLU_LIMIT, SWIGLU_LIMIT)
        act = (up + 1.0) * (gl * jax.nn.sigmoid(SWIGLU_ALPHA * gl))
        y = _dot(act.astype(BF16), wdn_bf_ref[...]) + bdn_ref[...]
        ys_ref[...] = w * y


def _expert_call(l, tb, xs, w_gu, b_gu, w_dn, b_dn):
    return pl.pallas_call(
        _expert_kernel,
        grid_spec=pltpu.PrefetchScalarGridSpec(
            num_scalar_prefetch=3,
            grid=(N_ET_MAX,),
            in_specs=[
                pl.BlockSpec((ET, XS_W), lambda j, te, tr, n: (tr[j], 0)),
                pl.BlockSpec((None, None, D, 2 * FF), lambda j, te, tr, n: (l, te[j], 0, 0)),
                pl.BlockSpec((None, None, 1, 2 * FF), lambda j, te, tr, n: (l, te[j], 0, 0)),
                pl.BlockSpec((None, None, FF, D), lambda j, te, tr, n: (l, te[j], 0, 0)),
                pl.BlockSpec((None, None, 1, D), lambda j, te, tr, n: (l, te[j], 0, 0)),
            ],
            out_specs=pl.BlockSpec((ET, D), lambda j, te, tr, n: (tr[j], 0)),
            scratch_shapes=[pltpu.VMEM((D, 2 * FF), BF16), pltpu.VMEM((FF, D), BF16)]),
        out_shape=jax.ShapeDtypeStruct((N_SLOTS, D), F32),
        compiler_params=_cparams(("arbitrary",)),
        name="moe_experts",
    )(tb["te"], tb["tr"], tb["n_et"], xs, w_gu, b_gu.reshape(DEPTH, N_EXP, 1, 2 * FF),
      w_dn, b_dn.reshape(DEPTH, N_EXP, 1, D))


def _combine_kernel(rd_ref, npass_ref, gt_ref, shift_ref, x1_ref, mod_ref, ys_ref, o_ref,
                    buf_ref, sem_ref):
    i = pl.program_id(0)
    n_tiles = pl.num_programs(0)

    def chunk_copy(tile, p, slot, e):
        src = rd_ref[(p * N_TILES + tile) * N_EXP + e]
        return pltpu.make_async_copy(ys_ref.at[pl.ds(pl.multiple_of(src, 8), CH)],
                                     buf_ref.at[slot, pl.ds(e * CH, CH)], sem_ref.at[slot])

    def start_reads(tile, p, slot):
        for e in range(N_EXP):
            chunk_copy(tile, p, slot, e).start()

    def wait_reads(slot):
        for e in range(N_EXP):
            chunk_copy(0, 0, slot, e).wait()

    slot = i & 1

    @pl.when(i == 0)
    def _():
        start_reads(0, 0, 0)

    @pl.when(i + 1 < n_tiles)
    def _():
        start_reads(i + 1, 0, 1 - slot)

    rank_t = _rank_in_tile(gt_ref[...])
    r = lax.broadcasted_iota(I32, (TM, TM), 0)
    c = lax.broadcasted_iota(I32, (TM, TM), 1)
    eye = (r == c).astype(BF16)
    esub = lax.broadcasted_iota(I32, (N_EXP, ET), 0)
    ecol = lax.broadcasted_iota(I32, (N_EXP, ET), 1)
    jcol = (lax.broadcasted_iota(I32, (TM, ET), 1) & (CH - 1)).astype(F32)

    def unsort(p):
        rel = rank_t - (p * CH).astype(F32)
        adj = jnp.where(jnp.logical_and(rel >= 0.0, rel < CH), rel + shift_ref[p][:, 0:1], -1e4)
        rk = _dot_nt(eye, adj.astype(BF16)).astype(BF16)
        acc = jnp.zeros((TM, D), F32)
        for cidx in range(SLOTS // ET):
            expand = (((ecol + cidx * ET) >> 6) == esub).astype(BF16)
            hit = _dot(rk, expand) == jcol
            y = buf_ref[slot, cidx * ET:(cidx + 1) * ET, :].astype(BF16)
            acc = acc + _dot(hit.astype(BF16), y)
        return acc

    wait_reads(slot)
    acc = unsort(jnp.int32(0))

    def extra_pass(p, acc):
        start_reads(i, p, slot)
        wait_reads(slot)
        return acc + unsort(p)

    acc = lax.fori_loop(1, npass_ref[i], extra_pass, acc)
    o_ref[...] = x1_ref[...] + mod_ref[...][:, 5 * D:6 * D] * acc


def _combine_call(l, tb, gate_t, x1, mods, ys):
    return pl.pallas_call(
        _combine_kernel,
        grid_spec=pltpu.PrefetchScalarGridSpec(
            num_scalar_prefetch=2,
            grid=(N_TILES,),
            in_specs=[
                pl.BlockSpec((N_EXP, TM), lambda i, *_: (0, i)),
                pl.BlockSpec((None, N_PASS_MAX, N_EXP, LANES), lambda i, *_: (i, 0, 0, 0)),
                pl.BlockSpec((TM, D), lambda i, *_: (i, 0)),
                pl.BlockSpec((None, 1, 6 * D), lambda i, *_: (l * 8 + _grp(i), 0, 0)),
                pl.BlockSpec(memory_space=pl.ANY),
            ],
            out_specs=pl.BlockSpec((TM, D), lambda i, *_: (i, 0)),
            scratch_shapes=[pltpu.VMEM((2, SLOTS, D), F32), pltpu.SemaphoreType.DMA((2,))]),
        out_shape=jax.ShapeDtypeStruct((T, D), F32),
        compiler_params=_cparams(("arbitrary",)),
        name="moe_combine",
    )(tb["rd"], tb["npass"], gate_t, tb["shift"], x1, mods, ys)


def _rope_tables():
    t = jnp.arange(S_LAT, dtype=I32)
    row = (t // GRID_W).astype(F32)
    col = (t % GRID_W).astype(F32)

    def table(h, reps):
        inv = ROPE_THETA ** (-jnp.arange(0, h, 2, dtype=F32) / h)
        ar = row[:, None] * inv[None, :]
        ac = col[:, None] * inv[None, :]
        cos = jnp.concatenate([jnp.cos(ar), jnp.cos(ar), jnp.cos(ac), jnp.cos(ac)], axis=-1)
        sin = jnp.concatenate([-jnp.sin(ar), jnp.sin(ar), -jnp.sin(ac), jnp.sin(ac)], axis=-1)
        cos = jnp.tile(cos, (1, reps))
        sin = jnp.tile(sin, (1, reps))
        ident_c = jnp.ones((TM, LANES), F32)
        ident_s = jnp.zeros((TM, LANES), F32)
        return jnp.concatenate([ident_c, cos], axis=0), jnp.concatenate([ident_s, sin], axis=0)

    c64, s64 = table(HD // 2, 2)
    c32, s32 = table(MLA_ROPE // 2, 4)
    return c64, s64, c32, s32


def _group_matrix(n_rows, row_group, n_cols, col_group):
    r = np.arange(n_rows)[:, None] // row_group
    c = np.arange(n_cols)[None, :] // col_group
    return jnp.asarray((r == c).astype(np.float32), dtype=BF16)


def kernel(x_prompt, x_sample, cache_gqa_k, cache_gqa_v, cache_mla_ckv, cache_mla_krope, c, c_ctx,
           w_ada, b_ada, norm1_g, w_in, gqa_qn_g, gqa_kn_g, w_o_gqa, mla_cq_g, w_uq, mla_ckv_g, w_ukv,
           mla_qn_g, mla_kn_g, mla_krn_g, w_o_mla, conv_dw, conv_dw_b, conv_ln_g, conv_ln_b, w_pw2,
           w_out, norm2_g, w_router, b_router, w_gu, b_gu, w_dn, b_dn):
    L = DEPTH
    c64, s64, c32, s32 = _rope_tables()
    consts = dict(c64=c64, s64=s64, c32=c32, s32=s32,
                  b64=_group_matrix(512, HD, 512, HD), brn=_group_matrix(256, MLA_ROPE, 512, HD),
                  bnr=_group_matrix(512, HD, 256, MLA_ROPE),
                  brr=_group_matrix(256, MLA_ROPE, 256, MLA_ROPE))

    pad = jnp.zeros((L, D, LANES - MLA_ROPE), F32)
    wa = jnp.concatenate([w_in[:, :, 0:1440], pad, w_in[:, :, 1440:2464]], axis=-1).astype(BF16)
    wg = w_in[:, :, 2464:].astype(BF16)
    uq = w_uq.reshape(L, MLA_Q_RANK, NH, MLA_QK)
    uq = jnp.concatenate([uq[..., :HD].reshape(L, MLA_Q_RANK, NH * HD),
                          uq[..., HD:].reshape(L, MLA_Q_RANK, NH * MLA_ROPE)], axis=-1).astype(BF16)
    ukv = w_ukv.reshape(L, MLA_KV_RANK, NH, 2 * HD)
    ukv = jnp.concatenate([ukv[..., :HD].reshape(L, MLA_KV_RANK, NH * HD),
                           ukv[..., HD:].reshape(L, MLA_KV_RANK, NH * HD)], axis=-1).astype(BF16)

    def row3(a):
        return a.reshape(L, 1, a.shape[-1])

    wr_t = jnp.swapaxes(w_router, 1, 2)
    wr_hi = wr_t.astype(BF16)
    lw = dict(
        norm1_g=row3(norm1_g), wa=wa, wg=wg, w_uq=uq, w_ukv=ukv,
        gqa_qn_g=row3(jnp.tile(gqa_qn_g, (1, NH))), gqa_kn_g=row3(jnp.tile(gqa_kn_g, (1, GQA_KV))),
        mla_cq_g=row3(mla_cq_g), mla_ckv_g=row3(mla_ckv_g),
        mla_krn_g=row3(jnp.concatenate([mla_krn_g, jnp.zeros((L, LANES - MLA_ROPE), F32)], axis=-1)),
        mla_qn_n=row3(jnp.tile(mla_qn_g[:, :HD], (1, NH))),
        mla_qn_r=row3(jnp.tile(mla_qn_g[:, HD:], (1, NH))),
        mla_kn_g=row3(jnp.tile(mla_kn_g, (1, NH))),
        conv_dw=conv_dw, conv_dw_b=row3(conv_dw_b), conv_ln_g=row3(conv_ln_g), conv_ln_b=row3(conv_ln_b),
        w_o_gqa=w_o_gqa.astype(BF16), w_o_mla=w_o_mla.astype(BF16), w_pw2=w_pw2.astype(BF16),
        w_out=w_out.astype(BF16), norm2_g=row3(norm2_g),
        wr_hi=wr_hi, wr_lo=(wr_t - wr_hi.astype(F32)).astype(BF16),
        b_router=b_router.reshape(L, N_EXP, 1),
    )

    cvec = jnp.concatenate([c_ctx[None, :], c, jnp.zeros((8 - 1 - N_LAT_B, D), F32)], axis=0)
    mods = _ada_call(cvec, w_ada, b_ada).reshape(L * 8, 1, 6 * D)

    ck = cache_gqa_k.reshape(N_LAT_B, L, PAST, GQA_KV * HD)
    cv_ = cache_gqa_v.reshape(N_LAT_B, L, PAST, GQA_KV * HD)
    kn_c, vm_c = _cache_kv_call(cache_mla_ckv, ukv, lw["mla_kn_g"], consts["b64"])

    x = jnp.concatenate([x_prompt.reshape(T_CTX, D), x_sample.reshape(T_LAT, D)], axis=0)
    new_k, new_v, new_ckv, new_kr = [], [], [], []
    for l in range(L):
        qg, kg, vg, qn, qr, ckv, kr, kn, vm, u = _inproj_call(l, x, mods, lw, consts)
        new_k.append(kg[:T_CTX].reshape(N_CTX_B, S_CTX, GQA_KV, HD))
        new_v.append(vg[:T_CTX].reshape(N_CTX_B, S_CTX, GQA_KV, HD))
        new_ckv.append(ckv[:T_CTX].reshape(N_CTX_B, S_CTX, MLA_KV_RANK))
        new_kr.append(kr[:T_CTX, :MLA_ROPE].reshape(N_CTX_B, S_CTX, MLA_ROPE))
        ag_c = _attn_call(l, False, False, qg, None, kg, vg, None, None, None, None)
        ag_l = _attn_call(l, True, False, qg, None, kg, vg, None, ck, cv_, None)
        am_c = _attn_call(l, False, True, qn, qr, kn, vm, kr, None, None, None)
        am_l = _attn_call(l, True, True, qn, qr, kn, vm, kr, kn_c, vm_c, cache_mla_krope)
        cv = _conv_call(l, u, lw)
        x1, h2, gate_t, cnt = _merge_call(l, x, mods, lw, ag_c, ag_l, am_c, am_l, cv)
        tb = _moe_tables(cnt[:, :, 0])
        xs = _dispatch_call(tb, gate_t, h2)
        ys = _expert_call(l, tb, xs, w_gu, b_gu, w_dn, b_dn)
        x = _combine_call(l, tb, gate_t, x1, mods, ys)

    return (x[:T_CTX].reshape(N_CTX_B, S_CTX, D), x[T_CTX:].reshape(N_LAT_B, S_LAT, D),
            jnp.stack(new_k, axis=1), jnp.stack(new_v, axis=1),
            jnp.stack(new_ckv, axis=1), jnp.stack(new_kr, axis=1))
```

```python
import functools

import numpy as np
import jax
import jax.numpy as jnp
from jax import lax
from jax.experimental import pallas as pl
from jax.experimental.pallas import tpu as pltpu

F32 = jnp.float32
BF16 = jnp.bfloat16
I32 = jnp.int32

D = 1024
DEPTH = 4
N_CTX_B, S_CTX = 16, 256
N_LAT_B, S_LAT = 2, 2048
PAST = 256
GRID_W = 64
ROPE_THETA = 10000.0
EPS = 1e-6
HD = 64
NH = 8
GQA_KV = 2
GQA_GROUP = NH // GQA_KV
MLA_Q_RANK = 384
MLA_KV_RANK = 256
MLA_ROPE = 32
MLA_QK = HD + MLA_ROPE
CONV_CH = 512
CONV_K = 31
CONV_PAD = CONV_K // 2
N_EXP = 32
TOP_K = 4
FF = 1024
SWIGLU_LIMIT = 7.0
SWIGLU_ALPHA = 1.702

T_CTX = N_CTX_B * S_CTX
T_LAT = N_LAT_B * S_LAT
T = T_CTX + T_LAT

LANES = 128
TM = 256
N_TILES = T // TM
CTX_TILES = T_CTX // TM
LAT_TILES_PER_B = S_LAT // TM
WA = 2560
GATES_W = 3 * D
SUB = 8
PIECE = 32
ET = 256
XS_W = D + LANES
SORT_ROWS = 1280
SORT_CHUNKS = SORT_ROWS // TM
assert TM * TOP_K + (SUB - 1) * N_EXP + PIECE - SUB <= SORT_ROWS
UNSORT_ROWS = 2048
UNSORT_CHUNKS = UNSORT_ROWS // TM
assert TM * TOP_K + (PIECE - 1) * N_EXP <= UNSORT_ROWS
N_PIECE_MAX = (TM * TOP_K + (SUB - 1) * N_EXP) // PIECE + N_EXP
FLUSH = ET + PIECE
SEG_SLACK = 2 * ET
N_ET_MAX = (T * TOP_K + (SUB - 1) * N_TILES * N_EXP + N_EXP * (PIECE - SUB)) // ET + N_EXP
N_SLOTS = N_ET_MAX * ET + N_EXP * SEG_SLACK
VMEM_LIMIT = 56 * 1024 * 1024


def _dot(a, b):
    return jnp.dot(a, b, preferred_element_type=F32)


def _dot_nt(a, b):
    return lax.dot_general(a, b, (((1,), (1,)), ((), ())), preferred_element_type=F32)


def _split3(x):
    hi = x.astype(BF16)
    r = x - hi.astype(F32)
    mid = r.astype(BF16)
    lo = (r - mid.astype(F32)).astype(BF16)
    return hi, mid, lo


def _cparams(sem, vmem=VMEM_LIMIT):
    return pltpu.CompilerParams(dimension_semantics=sem, vmem_limit_bytes=vmem)


def _grp(i):
    return jnp.where(i < CTX_TILES, 0, 1 + (i - CTX_TILES) // LAT_TILES_PER_B)


def _pos_blk(i):
    return jnp.where(i < CTX_TILES, 0, 1 + (i - CTX_TILES) % LAT_TILES_PER_B)


def _ada_kernel(c_ref, w_ref, b_ref, o_ref):
    c = c_ref[...]
    s = c * jax.nn.sigmoid(c)
    w = w_ref[...]
    s_hi, s_lo, _ = _split3(s)
    w_hi, w_lo, _ = _split3(w)
    o_ref[...] = _dot(s_hi, w_hi) + _dot(s_hi, w_lo) + _dot(s_lo, w_hi) + b_ref[...]


def _ada_call(cvec, w_ada, b_ada):
    tn = 1536
    return pl.pallas_call(
        _ada_kernel,
        grid=(DEPTH, 6 * D // tn),
        in_specs=[
            pl.BlockSpec((8, D), lambda l, j: (0, 0)),
            pl.BlockSpec((None, D, tn), lambda l, j: (l, 0, j)),
            pl.BlockSpec((None, 1, tn), lambda l, j: (l, 0, j)),
        ],
        out_specs=pl.BlockSpec((None, 8, tn), lambda l, j: (l, 0, j)),
        out_shape=jax.ShapeDtypeStruct((DEPTH, 8, 6 * D), F32),
        compiler_params=_cparams(("arbitrary", "arbitrary")),
        name="ada_mod",
    )(cvec, w_ada, b_ada.reshape(DEPTH, 1, 6 * D))


def _norm_mod(x, g, scale, shift):
    ms = jnp.mean(x * x, axis=-1, keepdims=True)
    return (x * lax.rsqrt(ms + EPS) * g) * (1.0 + scale) + shift


def _group_rms(x, ss, n, g):
    return x * lax.rsqrt(ss * (1.0 / n) + EPS) * g


def _partner(x, dist):
    n = x.shape[-1]
    lane = lax.broadcasted_iota(I32, x.shape, 1)
    r1 = pltpu.roll(x, dist, 1)
    r2 = pltpu.roll(x, n - dist, 1)
    r1_is_minus = pltpu.roll(lane, dist, 1) == ((lane + (n - dist)) & (n - 1))
    plus = jnp.where(r1_is_minus, r2, r1)
    minus = jnp.where(r1_is_minus, r1, r2)
    return jnp.where((lane & (2 * dist - 1)) < dist, plus, minus)


def _rope(x, cos, sin_signed, dist):
    return x * cos + _partner(x, dist) * sin_signed


def _tile_lanes(x, reps):
    return jnp.concatenate([x] * reps, axis=1)


def _inproj_kernel(x_ref, mod_ref, g1_ref, wa_ref, wuq_ref, wukv_ref,
                   qng_ref, kng_ref, cqg_ref, ckvg_ref, krg_ref, mqn_ref, mqr_ref, mkn_ref,
                   c64_ref, s64_ref, c32_ref, s32_ref,
                   b64_ref, brn_ref, bnr_ref, brr_ref,
                   qg_ref, kg_ref, vg_ref, qn_ref, qr_ref, ckv_ref, kr_ref, kn_ref, vm_ref, u_ref):
    mod = mod_ref[...]
    h = _norm_mod(x_ref[...], g1_ref[...], mod[:, D:2 * D], mod[:, 0:D])
    p = _dot(h.astype(BF16), wa_ref[...])
    b64 = b64_ref[...]
    c64 = c64_ref[...]
    s64 = s64_ref[...]
    c32 = c32_ref[...]
    s32 = s32_ref[...]

    q = p[:, 0:512]
    q = _group_rms(q, _dot((q * q).astype(BF16), b64), HD, qng_ref[...])
    q = _rope(q, _tile_lanes(c64, 4), _tile_lanes(s64, 4), 16)
    qg_ref[...] = (q * (HD ** -0.5)).astype(BF16)
    k = p[:, 512:640]
    k = _group_rms(k, _dot((k * k).astype(BF16), b64[0:128, 0:128]), HD, kng_ref[...])
    kg_ref[...] = _rope(k, c64, s64, 16)
    vg_ref[...] = p[:, 640:768]

    cq = p[:, 768:1152]
    cq = cq * lax.rsqrt(jnp.mean(cq * cq, axis=-1, keepdims=True) + EPS) * cqg_ref[...]
    qm = _dot(cq.astype(BF16), wuq_ref[...])
    qmn = qm[:, 0:512]
    qmr = qm[:, 512:768]
    sqn = (qmn * qmn).astype(BF16)
    sqr = (qmr * qmr).astype(BF16)
    ss_n = _dot(sqn, b64) + _dot(sqr, brn_ref[...])
    ss_r = _dot(sqn, bnr_ref[...]) + _dot(sqr, brr_ref[...])
    scale = MLA_QK ** -0.5
    qn_ref[...] = (_group_rms(qmn, ss_n, MLA_QK, mqn_ref[...]) * scale).astype(BF16)
    qmr = _group_rms(qmr, ss_r, MLA_QK, mqr_ref[...])
    qr_ref[...] = (_rope(qmr, _tile_lanes(c32, 2), _tile_lanes(s32, 2), 8) * scale).astype(BF16)

    ckv = p[:, 1152:1408]
    ckv = ckv * lax.rsqrt(jnp.mean(ckv * ckv, axis=-1, keepdims=True) + EPS) * ckvg_ref[...]
    ckv_ref[...] = ckv
    kv = _dot(ckv.astype(BF16), wukv_ref[...])
    kn = kv[:, 0:512]
    kn_ref[...] = _group_rms(kn, _dot((kn * kn).astype(BF16), b64), HD, mkn_ref[...]).astype(BF16)
    vm_ref[...] = kv[:, 512:1024].astype(BF16)

    kr = p[:, 1408:1536]
    kr = kr * lax.rsqrt(jnp.sum(kr * kr, axis=-1, keepdims=True) * (1.0 / MLA_ROPE) + EPS) * krg_ref[...]
    kr_ref[...] = _rope(kr, c32, s32, 8)

    u_ref[...] = p[:, 1536:2048] * jax.nn.sigmoid(p[:, 2048:2560])


def _inproj_call(l, x, mods, lw, consts):
    def full(a):
        return pl.BlockSpec(a.shape, lambda i: (0,) * a.ndim)

    def layer(a):
        return pl.BlockSpec((None,) + a.shape[1:], lambda i: (l,) + (0,) * (a.ndim - 1))

    def rows(w):
        return pl.BlockSpec((TM, w), lambda i: (i, 0))

    def tab():
        return pl.BlockSpec((TM, LANES), lambda i: (_pos_blk(i), 0))

    layer_ws = [lw["norm1_g"], lw["wa"], lw["w_uq"], lw["w_ukv"], lw["gqa_qn_g"], lw["gqa_kn_g"],
                lw["mla_cq_g"], lw["mla_ckv_g"], lw["mla_krn_g"], lw["mla_qn_n"], lw["mla_qn_r"],
                lw["mla_kn_g"]]
    tabs = [consts["c64"], consts["s64"], consts["c32"], consts["s32"]]
    bms = [consts["b64"], consts["brn"], consts["bnr"], consts["brr"]]
    out_w = [(512, BF16), (128, F32), (128, F32), (512, BF16), (256, BF16), (256, F32), (128, F32),
             (512, BF16), (512, BF16), (512, F32)]
    return pl.pallas_call(
        _inproj_kernel,
        grid=(N_TILES,),
        in_specs=[rows(D), pl.BlockSpec((None, 1, 6 * D), lambda i: (l * 8 + _grp(i), 0, 0))]
        + [layer(a) for a in layer_ws] + [tab() for _ in tabs] + [full(a) for a in bms],
        out_specs=[rows(w) for w, _ in out_w],
        out_shape=[jax.ShapeDtypeStruct((T, w), dt) for w, dt in out_w],
        compiler_params=_cparams(("arbitrary",)),
        name="inproj",
    )(x, mods, *layer_ws, *tabs, *bms)


def _cache_kv_kernel(ckv_ref, wukv_ref, mkn_ref, b64_ref, kn_ref, vm_ref):
    kv = _dot(ckv_ref[...].astype(BF16), wukv_ref[...])
    kn = kv[:, 0:512]
    kn_ref[...] = _group_rms(kn, _dot((kn * kn).astype(BF16), b64_ref[...]), HD, mkn_ref[...]).astype(BF16)
    vm_ref[...] = kv[:, 512:1024].astype(BF16)


def _cache_kv_call(cache_ckv, w_ukv, mla_kn_g, b64):
    out = pl.BlockSpec((None, None, PAST, 512), lambda b, l: (b, l, 0, 0))
    return pl.pallas_call(
        _cache_kv_kernel,
        grid=(N_LAT_B, DEPTH),
        in_specs=[
            pl.BlockSpec((None, None, PAST, MLA_KV_RANK), lambda b, l: (b, l, 0, 0)),
            pl.BlockSpec((None, MLA_KV_RANK, 1024), lambda b, l: (l, 0, 0)),
            pl.BlockSpec((None, 1, 512), lambda b, l: (l, 0, 0)),
            pl.BlockSpec((512, 512), lambda b, l: (0, 0)),
        ],
        out_specs=[out, out],
        out_shape=[jax.ShapeDtypeStruct((N_LAT_B, DEPTH, PAST, 512), BF16)] * 2,
        compiler_params=_cparams(("arbitrary", "arbitrary")),
        name="cache_kv",
    )(cache_ckv, w_ukv, mla_kn_g, b64)


def _attn_kernel(*refs, mla, has_cache):
    it = iter(refs)
    q_ref = next(it)
    qr_ref = next(it) if mla else None
    ks_ref = next(it)
    vs_ref = next(it)
    krs_ref = next(it) if mla else None
    kc_ref = vc_ref = krc_ref = None
    if has_cache:
        kc_ref = next(it)
        vc_ref = next(it)
        krc_ref = next(it) if mla else None
    o_ref = next(it)

    if mla:
        krs = krs_ref[:, 0:MLA_ROPE].astype(BF16)
        krc = krc_ref[...].astype(BF16) if has_cache else None
    outs = []
    for h in range(NH):
        kvh = h if mla else h // GQA_GROUP
        sl = slice(HD * kvh, HD * kvh + HD)
        qh = q_ref[:, HD * h:HD * h + HD]
        s = _dot_nt(qh, ks_ref[:, sl].astype(BF16))
        if mla:
            qrh = qr_ref[:, MLA_ROPE * h:MLA_ROPE * h + MLA_ROPE]
            s = s + _dot_nt(qrh, krs)
        m = jnp.max(s, axis=-1, keepdims=True)
        if has_cache:
            sc = _dot_nt(qh, kc_ref[:, sl].astype(BF16))
            if mla:
                sc = sc + _dot_nt(qrh, krc)
            m = jnp.maximum(m, jnp.max(sc, axis=-1, keepdims=True))
        p = jnp.exp(s - m)
        den = jnp.sum(p, axis=-1, keepdims=True)
        o = _dot(p.astype(BF16), vs_ref[:, sl].astype(BF16))
        if has_cache:
            pc = jnp.exp(sc - m)
            den = den + jnp.sum(pc, axis=-1, keepdims=True)
            o = o + _dot(pc.astype(BF16), vc_ref[:, sl].astype(BF16))
        outs.append((o / den).astype(BF16))
    o_ref[...] = jnp.concatenate(outs, axis=1)


def _attn_call(l, latent, mla, q, qr, ks, vs, krs, kc, vc, krc):
    kw = ks.shape[1]
    if latent:
        nb, nq, n_self, q0, k0 = N_LAT_B, S_LAT // TM, S_LAT, CTX_TILES, T_CTX // S_LAT
    else:
        nb, nq, n_self, q0, k0 = N_CTX_B, 1, S_CTX, 0, 0

    def qspec(w):
        return pl.BlockSpec((TM, w), lambda b, i: (q0 + b * nq + i, 0))

    def kspec(w):
        return pl.BlockSpec((n_self, w), lambda b, i: (k0 + b, 0))

    def cspec(w):
        return pl.BlockSpec((None, None, PAST, w), lambda b, i: (b, l, 0, 0))

    args, specs = [q], [qspec(512)]
    if mla:
        args.append(qr)
        specs.append(qspec(256))
    args += [ks, vs]
    specs += [kspec(kw), kspec(kw)]
    if mla:
        args.append(krs)
        specs.append(kspec(LANES))
    if latent:
        args += [kc, vc]
        specs += [cspec(kw), cspec(kw)]
        if mla:
            args.append(krc)
            specs.append(cspec(MLA_ROPE))
    return pl.pallas_call(
        functools.partial(_attn_kernel, mla=mla, has_cache=latent),
        grid=(nb, nq),
        in_specs=specs,
        out_specs=pl.BlockSpec((TM, 512), lambda b, i: (b * nq + i, 0)),
        out_shape=jax.ShapeDtypeStruct((nb * nq * TM, 512), BF16),
        compiler_params=_cparams(("arbitrary", "arbitrary")),
        name=("mla" if mla else "gqa") + ("_lat" if latent else "_ctx"),
    )(*args)


def _conv_kernel(up_ref, uc_ref, un_ref, dw_ref, dwb_ref, lng_ref, lnb_ref, o_ref, win_ref):
    i = pl.program_id(0)
    in_seq = (i - CTX_TILES) % LAT_TILES_PER_B
    has_prev = jnp.logical_and(i >= CTX_TILES, in_seq != 0)
    has_next = jnp.logical_and(i >= CTX_TILES, in_seq != LAT_TILES_PER_B - 1)
    win_ref[0:16, :] = jnp.where(has_prev, up_ref[TM - 16:TM, :], 0.0)
    win_ref[16:16 + TM, :] = uc_ref[...]
    win_ref[16 + TM:32 + TM, :] = jnp.where(has_next, un_ref[0:16, :], 0.0)
    acc = jnp.zeros((TM, CONV_CH), F32) + dwb_ref[...]
    for j in range(CONV_K):
        acc = acc + win_ref[j + 1:j + 1 + TM, :] * dw_ref[j:j + 1, :]
    mu = jnp.mean(acc, axis=-1, keepdims=True)
    cen = acc - mu
    var = jnp.mean(cen * cen, axis=-1, keepdims=True)
    y = cen * lax.rsqrt(var + EPS) * lng_ref[...] + lnb_ref[...]
    o_ref[...] = (y * jax.nn.sigmoid(y)).astype(BF16)


def _conv_call(l, u, lw):
    def layer(a):
        return pl.BlockSpec((None,) + a.shape[1:], lambda i: (l,) + (0,) * (a.ndim - 1))

    ws = [lw["conv_dw"], lw["conv_dw_b"], lw["conv_ln_g"], lw["conv_ln_b"]]
    return pl.pallas_call(
        _conv_kernel,
        grid=(N_TILES,),
        in_specs=[
            pl.BlockSpec((TM, CONV_CH), lambda i: (jnp.maximum(i - 1, 0), 0)),
            pl.BlockSpec((TM, CONV_CH), lambda i: (i, 0)),
            pl.BlockSpec((TM, CONV_CH), lambda i: (jnp.minimum(i + 1, N_TILES - 1), 0)),
        ] + [layer(a) for a in ws],
        out_specs=pl.BlockSpec((TM, CONV_CH), lambda i: (i, 0)),
        out_shape=jax.ShapeDtypeStruct((T, CONV_CH), BF16),
        scratch_shapes=[pltpu.VMEM((TM + 32, CONV_CH), F32)],
        compiler_params=_cparams(("arbitrary",)),
        name="conv",
    )(u, u, u, *ws)


def _merge_kernel(x_ref, mod_ref, g1_ref, wg_ref, agc_ref, agl_ref, amc_ref, aml_ref, cv_ref,
                  wog_ref, wom_ref, wpw_ref, wout_ref, g2_ref, wr_hi_ref, wr_lo_ref, br_ref,
                  x1_ref, h2_ref, gt_ref, cnt_ref):
    i = pl.program_id(0)
    is_ctx = i < CTX_TILES
    x = x_ref[...]
    mod = mod_ref[...]
    h = _norm_mod(x, g1_ref[...], mod[:, D:2 * D], mod[:, 0:D]).astype(BF16)
    gates = jax.nn.sigmoid(_dot(h, wg_ref[...]))
    ag = jnp.where(is_ctx, agc_ref[...], agl_ref[...])
    am = jnp.where(is_ctx, amc_ref[...], aml_ref[...])
    merged = (gates[:, 0:D] * _dot(ag, wog_ref[...])
              + gates[:, D:2 * D] * _dot(am, wom_ref[...])
              + gates[:, 2 * D:3 * D] * _dot(cv_ref[...], wpw_ref[...]))
    x1 = x + mod[:, 2 * D:3 * D] * _dot(merged.astype(BF16), wout_ref[...])
    x1_ref[...] = x1
    h2 = _norm_mod(x1, g2_ref[...], mod[:, 4 * D:5 * D], mod[:, 3 * D:4 * D])
    h2_hi = h2.astype(BF16)
    h2_ref[...] = h2_hi
    h2_lo = (h2 - h2_hi.astype(F32)).astype(BF16)
    wr_hi = wr_hi_ref[...]
    logit = _dot_nt(wr_hi, h2_hi) + _dot_nt(wr_hi, h2_lo) + _dot_nt(wr_lo_ref[...], h2_hi) + br_ref[...]
    row = lax.broadcasted_iota(I32, logit.shape, 0).astype(F32)
    sels, vals = [], []
    for _ in range(TOP_K):
        m = jnp.max(logit, axis=0, keepdims=True)
        idx = jnp.min(jnp.where(logit == m, row, float(N_EXP)), axis=0, keepdims=True)
        sel = row == idx
        sels.append(sel)
        vals.append(m)
        logit = jnp.where(sel, -jnp.inf, logit)
    es = [jnp.exp(v - vals[0]) for v in vals]
    den = es[0] + es[1] + es[2] + es[3]
    gate = jnp.zeros(logit.shape, F32)
    for sel, e in zip(sels, es):
        gate = jnp.where(sel, e / den, gate)
    gt_ref[...] = gate
    n = jnp.sum((gate > 0.0).astype(F32), axis=1, keepdims=True)
    cnt_ref[...] = jnp.broadcast_to(n, (N_EXP, LANES)).astype(I32)


def _merge_call(l, x, mods, lw, ag_c, ag_l, am_c, am_l, cv):
    def layer(a):
        return pl.BlockSpec((None,) + a.shape[1:], lambda i: (l,) + (0,) * (a.ndim - 1))

    def rows(w):
        return pl.BlockSpec((TM, w), lambda i: (i, 0))

    ctx_rows = pl.BlockSpec((TM, 512), lambda i: (jnp.minimum(i, CTX_TILES - 1), 0))
    lat_rows = pl.BlockSpec((TM, 512), lambda i: (jnp.maximum(i - CTX_TILES, 0), 0))
    ws1 = [lw["norm1_g"], lw["wg"]]
    ws2 = [lw["w_o_gqa"], lw["w_o_mla"], lw["w_pw2"], lw["w_out"], lw["norm2_g"],
           lw["wr_hi"], lw["wr_lo"], lw["b_router"]]
    return pl.pallas_call(
        _merge_kernel,
        grid=(N_TILES,),
        in_specs=[rows(D), pl.BlockSpec((None, 1, 6 * D), lambda i: (l * 8 + _grp(i), 0, 0))]
        + [layer(a) for a in ws1] + [ctx_rows, lat_rows, ctx_rows, lat_rows, rows(512)]
        + [layer(a) for a in ws2],
        out_specs=[rows(D), rows(D), pl.BlockSpec((N_EXP, TM), lambda i: (0, i)),
                   pl.BlockSpec((None, N_EXP, LANES), lambda i: (i, 0, 0))],
        out_shape=[jax.ShapeDtypeStruct((T, D), F32), jax.ShapeDtypeStruct((T, D), BF16),
                   jax.ShapeDtypeStruct((N_EXP, T), F32),
                   jax.ShapeDtypeStruct((N_TILES, N_EXP, LANES), I32)],
        compiler_params=_cparams(("arbitrary",)),
        name="merge",
    )(x, mods, *ws1, ag_c, ag_l, am_c, am_l, cv, *ws2)


def _moe_tables(cnt):
    n8 = (cnt + SUB - 1) // SUB * SUB
    loc = jnp.cumsum(n8, axis=1) - n8
    tot = jnp.sum(n8, axis=0)
    ntile = (tot + PIECE - SUB + ET - 1) // ET
    seg = ntile * ET + SEG_SLACK
    off = jnp.cumsum(seg) - seg
    start = off[None, :] + jnp.cumsum(n8, axis=0) - n8
    npc = (n8 + PIECE - 1) // PIECE
    pc = jnp.cumsum(npc, axis=1)
    p = jnp.arange(N_PIECE_MAX, dtype=I32)
    pe = jnp.minimum(jnp.sum((pc[:, None, :] <= p[None, :, None]).astype(I32), axis=2), N_EXP - 1)
    k = p[None, :] - jnp.take_along_axis(pc - npc, pe, axis=1)
    src = jnp.take_along_axis(loc, pe, axis=1) + PIECE * k
    dst = jnp.take_along_axis(start, pe, axis=1) + PIECE * k
    loc_c = jnp.cumsum(npc, axis=1) * PIECE - npc * PIECE
    src_c = jnp.take_along_axis(loc_c, pe, axis=1) + PIECE * k
    live = p[None, :] < pc[:, -1:]

    def col_table(a):
        return jnp.broadcast_to(a.astype(F32)[:, :, None], (N_TILES, N_EXP, LANES))

    def row_table(a):
        a = jnp.concatenate([a.astype(F32), jnp.zeros((N_TILES, LANES - N_EXP), F32)], axis=1)
        return jnp.broadcast_to(a[:, None, :], (N_TILES, SUB, LANES))

    return dict(src=jnp.where(live, src, 0).reshape(-1).astype(I32),
                src_c=jnp.where(live, src_c, 0).reshape(-1).astype(I32),
                dst=jnp.where(live, dst, 0).reshape(-1).astype(I32),
                n_piece=pc[:, -1].astype(I32), flush=(off + tot).astype(I32),
                first=(off // ET).astype(I32), ntile=ntile.astype(I32),
                lo_row=row_table(loc), hi_row=row_table(loc + n8),
                lo_col=col_table(loc_c), hi_col=col_table(loc_c + n8))


def _rank_in_tile(gate_t):
    sel = gate_t > 0.0
    r = lax.broadcasted_iota(I32, (TM, TM), 0)
    c = lax.broadcasted_iota(I32, (TM, TM), 1)
    upper = (r < c).astype(BF16)
    rank = _dot(sel.astype(BF16), upper)
    return jnp.where(sel, rank, -1e4)


def _piece_loop(n, body):
    def step(p, carry):
        body(p)
        return carry

    lax.fori_loop(0, n, step, 0)


def _dispatch_kernel(src_ref, dst_ref, npc_ref, flush_ref, gt_ref, lo_ref, hi_ref, h2_ref, xs_ref,
                     buf_ref, zero_ref, sem_ref, zsem_ref):
    i = pl.program_id(0)
    slot = i & 1

    def piece_copy(tile, p, s):
        src = src_ref[tile * N_PIECE_MAX + p]
        dst = dst_ref[tile * N_PIECE_MAX + p]
        return pltpu.make_async_copy(buf_ref.at[s, pl.ds(pl.multiple_of(src, SUB), PIECE)],
                                     xs_ref.at[pl.ds(pl.multiple_of(dst, SUB), PIECE)], sem_ref.at[s])

    @pl.when(i == 0)
    def _():
        zero_ref[...] = jnp.zeros(zero_ref.shape, F32)
        copies = [pltpu.make_async_copy(
            zero_ref, xs_ref.at[pl.ds(pl.multiple_of(flush_ref[e], SUB), FLUSH)], zsem_ref.at[0])
            for e in range(N_EXP)]
        for cp in copies:
            cp.start()
        for cp in copies:
            cp.wait()

    gate_t = gt_ref[...]
    rk = _rank_in_tile(gate_t).astype(BF16)
    g_hi, g_mid, g_lo = _split3(gate_t)
    h2 = h2_ref[...]
    lo = lo_ref[0:1, 0:N_EXP]
    hi = hi_ref[0:1, 0:N_EXP]
    for c in range(SORT_CHUNKS):
        r = (lax.broadcasted_iota(I32, (TM, N_EXP), 0) + c * TM).astype(F32)
        own = jnp.logical_and(r >= lo, r < hi)
        base = jnp.sum(jnp.where(own, lo, 0.0), axis=1, keepdims=True)
        owned = jnp.sum(jnp.where(own, 1.0, 0.0), axis=1, keepdims=True) > 0.0
        in_run = jnp.where(owned, r[:, 0:1] - base, -1.0)
        expand = own.astype(BF16)
        hit = _dot(expand, rk) == in_run
        gexp = _dot(expand, g_hi) + _dot(expand, g_mid) + _dot(expand, g_lo)
        w = jnp.sum(jnp.where(hit, gexp, 0.0), axis=1, keepdims=True)
        buf_ref[slot, c * TM:(c + 1) * TM, 0:D] = _dot(hit.astype(BF16), h2)
        buf_ref[slot, c * TM:(c + 1) * TM, D:XS_W] = jnp.broadcast_to(w, (TM, LANES))

    @pl.when(i > 0)
    def _():
        _piece_loop(npc_ref[i - 1], lambda p: piece_copy(i - 1, p, 1 - slot).wait())

    _piece_loop(npc_ref[i], lambda p: piece_copy(i, p, slot).start())

    @pl.when(i == pl.num_programs(0) - 1)
    def _():
        _piece_loop(npc_ref[i], lambda p: piece_copy(i, p, slot).wait())


def _dispatch_call(tb, gate_t, h2):
    tab = pl.BlockSpec((None, SUB, LANES), lambda i, *_: (i, 0, 0))
    return pl.pallas_call(
        _dispatch_kernel,
        grid_spec=pltpu.PrefetchScalarGridSpec(
            num_scalar_prefetch=4,
            grid=(N_TILES,),
            in_specs=[pl.BlockSpec((N_EXP, TM), lambda i, *_: (0, i)), tab, tab,
                      pl.BlockSpec((TM, D), lambda i, *_: (i, 0))],
            out_specs=pl.BlockSpec(memory_space=pl.ANY),
            scratch_shapes=[pltpu.VMEM((2, SORT_ROWS, XS_W), F32), pltpu.VMEM((FLUSH, XS_W), F32),
                            pltpu.SemaphoreType.DMA((2,)), pltpu.SemaphoreType.DMA((1,))]),
        out_shape=jax.ShapeDtypeStruct((N_SLOTS, XS_W), F32),
        compiler_params=_cparams(("arbitrary",)),
        name="moe_dispatch",
    )(tb["src"], tb["dst"], tb["n_piece"], tb["flush"], gate_t, tb["lo_row"], tb["hi_row"], h2)


def _expert_kernel(first_ref, ntile_ref, bgu_ref, bdn_ref, xs_ref, wgu_ref, wdn_ref, ys_ref,
                   wgu_f32_ref, wdn_f32_ref, wgu_bf_ref, wdn_bf_ref, x_buf_ref, y_buf_ref,
                   wsem_ref, xsem_ref, ysem_ref, *, layer):
    e = pl.program_id(0)
    slot = e & 1

    def weight_copies(expert, s):
        return (pltpu.make_async_copy(wgu_ref.at[layer, expert], wgu_f32_ref.at[s], wsem_ref.at[0, s]),
                pltpu.make_async_copy(wdn_ref.at[layer, expert], wdn_f32_ref.at[s], wsem_ref.at[1, s]))

    def x_copy(t, s):
        row = pl.multiple_of((first_ref[e] + t) * ET, ET)
        return pltpu.make_async_copy(xs_ref.at[pl.ds(row, ET)], x_buf_ref.at[s], xsem_ref.at[s])

    def y_copy(t, s):
        row = pl.multiple_of((first_ref[e] + t) * ET, ET)
        return pltpu.make_async_copy(y_buf_ref.at[s], ys_ref.at[pl.ds(row, ET)], ysem_ref.at[s])

    @pl.when(e == 0)
    def _():
        for cp in weight_copies(0, 0):
            cp.start()

    @pl.when(e + 1 < pl.num_programs(0))
    def _():
        for cp in weight_copies(e + 1, 1 - slot):
            cp.start()

    n_t = ntile_ref[e]

    @pl.when(n_t > 0)
    def _():
        x_copy(0, 0).start()

    for cp in weight_copies(e, slot):
        cp.wait()
    for r in range(0, D, 128):
        wgu_bf_ref[r:r + 128, :] = wgu_f32_ref[slot, r:r + 128, :].astype(BF16)
    for r in range(0, FF, 128):
        wdn_bf_ref[r:r + 128, :] = wdn_f32_ref[slot, r:r + 128, :].astype(BF16)
    bgu = bgu_ref[...]
    bdn = bdn_ref[...]

    def row_tile(t, carry):
        s = t & 1

        @pl.when(t + 1 < n_t)
        def _():
            x_copy(t + 1, 1 - s).start()

        x_copy(t, s).wait()

        @pl.when(t >= 2)
        def _():
            y_copy(t - 2, s).wait()

        x = x_buf_ref[s, :, 0:D].astype(BF16)
        w = x_buf_ref[s, :, D:D + 1]
        gu = _dot(x, wgu_bf_ref[...]) + bgu
        gl = jnp.minimum(gu[:, 0:FF], SWIGLU_LIMIT)
        up = jnp.clip(gu[:, FF:2 * FF], -SWIGLU_LIMIT, SWIGLU_LIMIT)
        act = (up + 1.0) * (gl * jax.nn.sigmoid(SWIGLU_ALPHA * gl))
        y_buf_ref[s] = w * (_dot(act.astype(BF16), wdn_bf_ref[...]) + bdn)
        y_copy(t, s).start()
        return carry

    lax.fori_loop(0, n_t, row_tile, 0)

    @pl.when(n_t >= 2)
    def _():
        y_copy(n_t - 2, n_t & 1).wait()

    @pl.when(n_t >= 1)
    def _():
        y_copy(n_t - 1, (n_t - 1) & 1).wait()


def _expert_call(l, tb, xs, w_gu, b_gu, w_dn, b_dn):
    return pl.pallas_call(
        functools.partial(_expert_kernel, layer=l),
        grid_spec=pltpu.PrefetchScalarGridSpec(
            num_scalar_prefetch=2,
            grid=(N_EXP,),
            in_specs=[
                pl.BlockSpec((None, None, 1, 2 * FF), lambda e, *_: (l, e, 0, 0)),
                pl.BlockSpec((None, None, 1, D), lambda e, *_: (l, e, 0, 0)),
                pl.BlockSpec(memory_space=pl.ANY),
                pl.BlockSpec(memory_space=pl.ANY),
                pl.BlockSpec(memory_space=pl.ANY),
            ],
            out_specs=pl.BlockSpec(memory_space=pl.ANY),
            scratch_shapes=[pltpu.VMEM((2, D, 2 * FF), F32), pltpu.VMEM((2, FF, D), F32),
                            pltpu.VMEM((D, 2 * FF), BF16), pltpu.VMEM((FF, D), BF16),
                            pltpu.VMEM((2, ET, XS_W), F32), pltpu.VMEM((2, ET, D), F32),
                            pltpu.SemaphoreType.DMA((2, 2)), pltpu.SemaphoreType.DMA((2,)),
                            pltpu.SemaphoreType.DMA((2,))]),
        out_shape=jax.ShapeDtypeStruct((N_SLOTS, D), F32),
        compiler_params=_cparams(("arbitrary",)),
        name="moe_experts",
    )(tb["first"], tb["ntile"], b_gu.reshape(DEPTH, N_EXP, 1, 2 * FF), b_dn.reshape(DEPTH, N_EXP, 1, D),
      xs, w_gu, w_dn)


def _combine_kernel(src_ref, dst_ref, npc_ref, gt_ref, lo_ref, hi_ref, x1_ref, mod_ref, ys_ref, o_ref,
                    buf_ref, sem_ref):
    i = pl.program_id(0)
    slot = i & 1

    def piece_copy(tile, p, s):
        src = src_ref[tile * N_PIECE_MAX + p]
        dst = dst_ref[tile * N_PIECE_MAX + p]
        return pltpu.make_async_copy(ys_ref.at[pl.ds(pl.multiple_of(dst, SUB), PIECE)],
                                     buf_ref.at[s, pl.ds(pl.multiple_of(src, SUB), PIECE)], sem_ref.at[s])

    @pl.when(i == 0)
    def _():
        buf_ref[...] = jnp.zeros(buf_ref.shape, F32)
        _piece_loop(npc_ref[0], lambda p: piece_copy(0, p, 0).start())

    @pl.when(i + 1 < pl.num_programs(0))
    def _():
        _piece_loop(npc_ref[i + 1], lambda p: piece_copy(i + 1, p, 1 - slot).start())

    rank_t = _rank_in_tile(gt_ref[...])
    r = lax.broadcasted_iota(I32, (TM, TM), 0)
    c = lax.broadcasted_iota(I32, (TM, TM), 1)
    eye = (r == c).astype(BF16)
    rk = _dot_nt(eye, rank_t.astype(BF16)).astype(BF16)
    lo = lo_ref[:, 0:1]
    hi = hi_ref[:, 0:1]

    _piece_loop(npc_ref[i], lambda p: piece_copy(i, p, slot).wait())
    acc = jnp.zeros((TM, D), F32)
    for cidx in range(UNSORT_CHUNKS):
        rr =(lax.broadcasted_iota(I32, (N_EXP, TM), 1) + cidx * TM).astype(F32)
        own = jnp.logical_and(rr >= lo, rr < hi)
        base = jnp.sum(jnp.where(own, lo, 0.0), axis=0, keepdims=True)
        owned = jnp.sum(jnp.where(own, 1.0, 0.0), axis=0, keepdims=True) > 0.0
        in_run = jnp.where(owned, rr[0:1, :] - base, -1.0)
        hit = _dot(rk, own.astype(BF16)) == in_run
        y = buf_ref[slot, cidx * TM:(cidx + 1) * TM, :].astype(BF16)
        acc = acc + _dot(hit.astype(BF16), y)
    o_ref[...] = x1_ref[...] + mod_ref[...][:, 5 * D:6 * D] * acc


def _combine_call(l, tb, gate_t, x1, mods, ys):
    tab = pl.BlockSpec((None, N_EXP, LANES), lambda i, *_: (i, 0, 0))
    return pl.pallas_call(
        _combine_kernel,
        grid_spec=pltpu.PrefetchScalarGridSpec(
            num_scalar_prefetch=3,
            grid=(N_TILES,),
            in_specs=[
                pl.BlockSpec((N_EXP, TM), lambda i, *_: (0, i)), tab, tab,
                pl.BlockSpec((TM, D), lambda i, *_: (i, 0)),
                pl.BlockSpec((None, 1, 6 * D), lambda i, *_: (l * 8 + _grp(i), 0, 0)),
                pl.BlockSpec(memory_space=pl.ANY),
            ],
            out_specs=pl.BlockSpec((TM, D), lambda i, *_: (i, 0)),
            scratch_shapes=[pltpu.VMEM((2, UNSORT_ROWS, D), F32), pltpu.SemaphoreType.DMA((2,))]),
        out_shape=jax.ShapeDtypeStruct((T, D), F32),
        compiler_params=_cparams(("arbitrary",)),
        name="moe_combine",
    )(tb["src_c"], tb["dst"], tb["n_piece"], gate_t, tb["lo_col"], tb["hi_col"], x1, mods, ys)


def _rope_tables():
    t = jnp.arange(S_LAT, dtype=I32)
    row = (t // GRID_W).astype(F32)
    col = (t % GRID_W).astype(F32)

    def table(h, reps):
        inv = ROPE_THETA ** (-jnp.arange(0, h, 2, dtype=F32) / h)
        ar = row[:, None] * inv[None, :]
        ac = col[:, None] * inv[None, :]
        cos = jnp.concatenate([jnp.cos(ar), jnp.cos(ar), jnp.cos(ac), jnp.cos(ac)], axis=-1)
        sin = jnp.concatenate([-jnp.sin(ar), jnp.sin(ar), -jnp.sin(ac), jnp.sin(ac)], axis=-1)
        cos = jnp.tile(cos, (1, reps))
        sin = jnp.tile(sin, (1, reps))
        ident_c = jnp.ones((TM, LANES), F32)
        ident_s = jnp.zeros((TM, LANES), F32)
        return jnp.concatenate([ident_c, cos], axis=0), jnp.concatenate([ident_s, sin], axis=0)

    c64, s64 = table(HD // 2, 2)
    c32, s32 = table(MLA_ROPE // 2, 4)
    return c64, s64, c32, s32


def _group_matrix(n_rows, row_group, n_cols, col_group):
    r = np.arange(n_rows)[:, None] // row_group
    c = np.arange(n_cols)[None, :] // col_group
    return jnp.asarray((r == c).astype(np.float32), dtype=BF16)


def kernel(x_prompt, x_sample, cache_gqa_k, cache_gqa_v, cache_mla_ckv, cache_mla_krope, c, c_ctx,
           w_ada, b_ada, norm1_g, w_in, gqa_qn_g, gqa_kn_g, w_o_gqa, mla_cq_g, w_uq, mla_ckv_g, w_ukv,
           mla_qn_g, mla_kn_g, mla_krn_g, w_o_mla, conv_dw, conv_dw_b, conv_ln_g, conv_ln_b, w_pw2,
           w_out, norm2_g, w_router, b_router, w_gu, b_gu, w_dn, b_dn):
    L = DEPTH
    c64, s64, c32, s32 = _rope_tables()
    consts = dict(c64=c64, s64=s64, c32=c32, s32=s32,
                  b64=_group_matrix(512, HD, 512, HD), brn=_group_matrix(256, MLA_ROPE, 512, HD),
                  bnr=_group_matrix(512, HD, 256, MLA_ROPE),
                  brr=_group_matrix(256, MLA_ROPE, 256, MLA_ROPE))

    pad = jnp.zeros((L, D, LANES - MLA_ROPE), F32)
    wa = jnp.concatenate([w_in[:, :, 0:1440], pad, w_in[:, :, 1440:2464]], axis=-1).astype(BF16)
    wg = w_in[:, :, 2464:].astype(BF16)
    uq = w_uq.reshape(L, MLA_Q_RANK, NH, MLA_QK)
    uq = jnp.concatenate([uq[..., :HD].reshape(L, MLA_Q_RANK, NH * HD),
                          uq[..., HD:].reshape(L, MLA_Q_RANK, NH * MLA_ROPE)], axis=-1).astype(BF16)
    ukv = w_ukv.reshape(L, MLA_KV_RANK, NH, 2 * HD)
    ukv = jnp.concatenate([ukv[..., :HD].reshape(L, MLA_KV_RANK, NH * HD),
                           ukv[..., HD:].reshape(L, MLA_KV_RANK, NH * HD)], axis=-1).astype(BF16)

    def row3(a):
        return a.reshape(L, 1, a.shape[-1])

    wr_t = jnp.swapaxes(w_router, 1, 2)
    wr_hi = wr_t.astype(BF16)
    lw = dict(
        norm1_g=row3(norm1_g), wa=wa, wg=wg, w_uq=uq, w_ukv=ukv,
        gqa_qn_g=row3(jnp.tile(gqa_qn_g, (1, NH))), gqa_kn_g=row3(jnp.tile(gqa_kn_g, (1, GQA_KV))),
        mla_cq_g=row3(mla_cq_g), mla_ckv_g=row3(mla_ckv_g),
        mla_krn_g=row3(jnp.concatenate([mla_krn_g, jnp.zeros((L, LANES - MLA_ROPE), F32)], axis=-1)),
        mla_qn_n=row3(jnp.tile(mla_qn_g[:, :HD], (1, NH))),
        mla_qn_r=row3(jnp.tile(mla_qn_g[:, HD:], (1, NH))),
        mla_kn_g=row3(jnp.tile(mla_kn_g, (1, NH))),
        conv_dw=conv_dw, conv_dw_b=row3(conv_dw_b), conv_ln_g=row3(conv_ln_g), conv_ln_b=row3(conv_ln_b),
        w_o_gqa=w_o_gqa.astype(BF16), w_o_mla=w_o_mla.astype(BF16), w_pw2=w_pw2.astype(BF16),
        w_out=w_out.astype(BF16), norm2_g=row3(norm2_g),
        wr_hi=wr_hi, wr_lo=(wr_t - wr_hi.astype(F32)).astype(BF16),
        b_router=b_router.reshape(L, N_EXP, 1),
    )

    cvec = jnp.concatenate([c_ctx[None, :], c, jnp.zeros((8 - 1 - N_LAT_B, D), F32)], axis=0)
    mods = _ada_call(cvec, w_ada, b_ada).reshape(L * 8, 1, 6 * D)

    ck = cache_gqa_k.reshape(N_LAT_B, L, PAST, GQA_KV * HD)
    cv_ = cache_gqa_v.reshape(N_LAT_B, L, PAST, GQA_KV * HD)
    kn_c, vm_c = _cache_kv_call(cache_mla_ckv, ukv, lw["mla_kn_g"], consts["b64"])

    x = jnp.concatenate([x_prompt.reshape(T_CTX, D), x_sample.reshape(T_LAT, D)], axis=0)
    new_k, new_v, new_ckv, new_kr = [], [], [], []
    for l in range(L):
        qg, kg, vg, qn, qr, ckv, kr, kn, vm, u = _inproj_call(l, x, mods, lw, consts)
        new_k.append(kg[:T_CTX].reshape(N_CTX_B, S_CTX, GQA_KV, HD))
        new_v.append(vg[:T_CTX].reshape(N_CTX_B, S_CTX, GQA_KV, HD))
        new_ckv.append(ckv[:T_CTX].reshape(N_CTX_B, S_CTX, MLA_KV_RANK))
        new_kr.append(kr[:T_CTX, :MLA_ROPE].reshape(N_CTX_B, S_CTX, MLA_ROPE))
        ag_c = _attn_call(l, False, False, qg, None, kg, vg, None, None, None, None)
        ag_l = _attn_call(l, True, False, qg, None, kg, vg, None, ck, cv_, None)
        am_c = _attn_call(l, False, True, qn, qr, kn, vm, kr, None, None, None)
        am_l = _attn_call(l, True, True, qn, qr, kn, vm, kr, kn_c, vm_c, cache_mla_krope)
        cv = _conv_call(l, u, lw)
        x1, h2, gate_t, cnt = _merge_call(l, x, mods, lw, ag_c, ag_l, am_c, am_l, cv)
        tb = _moe_tables(cnt[:, :, 0])
        xs = _dispatch_call(tb, gate_t, h2)
        ys = _expert_call(l, tb, xs, w_gu, b_gu, w_dn, b_dn)
        x = _combine_call(l, tb, gate_t, x1, mods, ys)

    return (x[:T_CTX].reshape(N_CTX_B, S_CTX, D), x[T_CTX:].reshape(N_LAT_B, S_LAT, D),
            jnp.stack(new_k, axis=1), jnp.stack(new_v, axis=1),
            jnp.stack(new_ckv, axis=1), jnp.stack(new_kr, axis=1))
```

```python
import functools

import numpy as np
import jax
import jax.numpy as jnp
from jax import lax
from jax.experimental import pallas as pl
from jax.experimental.pallas import tpu as pltpu

F32 = jnp.float32
BF16 = jnp.bfloat16
I32 = jnp.int32

D = 1024
DEPTH = 4
N_CTX_B, S_CTX = 16, 256
N_LAT_B, S_LAT = 2, 2048
PAST = 256
GRID_W = 64
ROPE_THETA = 10000.0
EPS = 1e-6
HD = 64
NH = 8
GQA_KV = 2
GQA_GROUP = NH // GQA_KV
MLA_Q_RANK = 384
MLA_KV_RANK = 256
MLA_ROPE = 32
MLA_QK = HD + MLA_ROPE
CONV_CH = 512
CONV_K = 31
CONV_PAD = CONV_K // 2
N_EXP = 32
TOP_K = 4
FF = 1024
SWIGLU_LIMIT = 7.0
SWIGLU_ALPHA = 1.702

T_CTX = N_CTX_B * S_CTX
T_LAT = N_LAT_B * S_LAT
T = T_CTX + T_LAT

LANES = 128
TM = 256
N_TILES = T // TM
CTX_TILES = T_CTX // TM
LAT_TILES_PER_B = S_LAT // TM
WA = 2560
GATES_W = 3 * D
SUB = 8
PIECE = 32
ET = 256
XS_W = D + LANES
SORT_ROWS = 1280
SORT_CHUNKS = SORT_ROWS // TM
assert TM * TOP_K + (SUB - 1) * N_EXP + PIECE - SUB <= SORT_ROWS
UNSORT_ROWS = 2048
assert TM * TOP_K + (PIECE - 1) * N_EXP <= UNSORT_ROWS
N_PIECE_MAX = (TM * TOP_K + (SUB - 1) * N_EXP) // PIECE + N_EXP
FLUSH = ET + PIECE
SEG_SLACK = 2 * ET
N_ET_MAX = (T * TOP_K + (SUB - 1) * N_TILES * N_EXP + N_EXP * (PIECE - SUB)) // ET + N_EXP
N_SLOTS = N_ET_MAX * ET + N_EXP * SEG_SLACK
VMEM_LIMIT = 56 * 1024 * 1024


def _dot(a, b):
    return jnp.dot(a, b, preferred_element_type=F32)


def _dot_nt(a, b):
    return lax.dot_general(a, b, (((1,), (1,)), ((), ())), preferred_element_type=F32)


def _split3(x):
    hi = x.astype(BF16)
    r = x - hi.astype(F32)
    mid = r.astype(BF16)
    lo = (r - mid.astype(F32)).astype(BF16)
    return hi, mid, lo


def _cparams(sem, vmem=VMEM_LIMIT):
    return pltpu.CompilerParams(dimension_semantics=sem, vmem_limit_bytes=vmem)


def _grp(i):
    return jnp.where(i < CTX_TILES, 0, 1 + (i - CTX_TILES) // LAT_TILES_PER_B)


def _pos_blk(i):
    return jnp.where(i < CTX_TILES, 0, 1 + (i - CTX_TILES) % LAT_TILES_PER_B)


def _ada_kernel(c_ref, w_ref, b_ref, o_ref):
    c = c_ref[...]
    s = c * jax.nn.sigmoid(c)
    w = w_ref[...]
    s_hi, s_lo, _ = _split3(s)
    w_hi, w_lo, _ = _split3(w)
    o_ref[...] = _dot(s_hi, w_hi) + _dot(s_hi, w_lo) + _dot(s_lo, w_hi) + b_ref[...]


def _ada_call(cvec, w_ada, b_ada):
    tn = 1536
    return pl.pallas_call(
        _ada_kernel,
        grid=(DEPTH, 6 * D // tn),
        in_specs=[
            pl.BlockSpec((8, D), lambda l, j: (0, 0)),
            pl.BlockSpec((None, D, tn), lambda l, j: (l, 0, j)),
            pl.BlockSpec((None, 1, tn), lambda l, j: (l, 0, j)),
        ],
        out_specs=pl.BlockSpec((None, 8, tn), lambda l, j: (l, 0, j)),
        out_shape=jax.ShapeDtypeStruct((DEPTH, 8, 6 * D), F32),
        compiler_params=_cparams(("arbitrary", "arbitrary")),
        name="ada_mod",
    )(cvec, w_ada, b_ada.reshape(DEPTH, 1, 6 * D))


def _norm_mod(x, g, scale, shift):
    ms = jnp.mean(x * x, axis=-1, keepdims=True)
    return (x * lax.rsqrt(ms + EPS) * g) * (1.0 + scale) + shift


def _group_rms(x, ss, n, g):
    return x * lax.rsqrt(ss * (1.0 / n) + EPS) * g


def _partner(x, dist):
    n = x.shape[-1]
    lane = lax.broadcasted_iota(I32, x.shape, 1)
    r1 = pltpu.roll(x, dist, 1)
    r2 = pltpu.roll(x, n - dist, 1)
    r1_is_minus = pltpu.roll(lane, dist, 1) == ((lane + (n - dist)) & (n - 1))
    plus = jnp.where(r1_is_minus, r2, r1)
    minus = jnp.where(r1_is_minus, r1, r2)
    return jnp.where((lane & (2 * dist - 1)) < dist, plus, minus)


def _rope(x, cos, sin_signed, dist):
    return x * cos + _partner(x, dist) * sin_signed


def _tile_lanes(x, reps):
    return jnp.concatenate([x] * reps, axis=1)


def _lanes_up(x, s):
    lane = lax.broadcasted_iota(I32, x.shape, 1)
    r1 = pltpu.roll(x, s, 1)
    r2 = pltpu.roll(x, LANES - s, 1)
    r1_moves_up = pltpu.roll(lane, s, 1) == ((lane + (LANES - s)) & (LANES - 1))
    return jnp.where(r1_moves_up, r1, r2)


def _mla_head_blocks(nope, rope_blocks):
    lane = lax.broadcasted_iota(I32, (nope.shape[0], LANES), 1)
    blocks = []
    for h in range(NH):
        pair = nope[:, LANES * (h // 2):LANES * (h // 2) + LANES]
        if h % 2 == 1:
            pair = pltpu.roll(pair, HD, 1)
        blocks.append(jnp.where(lane < HD, pair, jnp.where(lane < MLA_QK, rope_blocks[h], 0.0)))
    return jnp.concatenate(blocks, axis=1).astype(BF16)


def _inproj_kernel(x_ref, mod_ref, g1_ref, wa_ref, wuq_ref, wukv_ref,
                   qng_ref, kng_ref, cqg_ref, ckvg_ref, krg_ref, mqn_ref, mqr_ref, mkn_ref,
                   c64_ref, s64_ref, c32_ref, s32_ref,
                   b64_ref, brn_ref, bnr_ref, brr_ref,
                   qg_ref, kg_ref, vg_ref, qm_ref, ckv_ref, kr_ref, km_ref, vm_ref, u_ref):
    mod = mod_ref[...]
    h = _norm_mod(x_ref[...], g1_ref[...], mod[:, D:2 * D], mod[:, 0:D])
    p = _dot(h.astype(BF16), wa_ref[...])
    b64 = b64_ref[...]
    c64 = c64_ref[...]
    s64 = s64_ref[...]
    c32 = c32_ref[...]
    s32 = s32_ref[...]

    q = p[:, 0:512]
    q = _group_rms(q, _dot((q * q).astype(BF16), b64), HD, qng_ref[...])
    q = _rope(q, _tile_lanes(c64, 4), _tile_lanes(s64, 4), 16)
    qg_ref[...] = (q * (HD ** -0.5)).astype(BF16)
    k = p[:, 512:640]
    k = _group_rms(k, _dot((k * k).astype(BF16), b64[0:128, 0:128]), HD, kng_ref[...])
    kg_ref[...] = _rope(k, c64, s64, 16)
    vg_ref[...] = p[:, 640:768]

    cq = p[:, 768:1152]
    cq = cq * lax.rsqrt(jnp.mean(cq * cq, axis=-1, keepdims=True) + EPS) * cqg_ref[...]
    qm = _dot(cq.astype(BF16), wuq_ref[...])
    qmn = qm[:, 0:512]
    qmr = qm[:, 512:768]
    sqn = (qmn * qmn).astype(BF16)
    sqr = (qmr * qmr).astype(BF16)
    ss_n = _dot(sqn, b64) + _dot(sqr, brn_ref[...])
    ss_r = _dot(sqn, bnr_ref[...]) + _dot(sqr, brr_ref[...])
    scale = MLA_QK ** -0.5
    qmn = _group_rms(qmn, ss_n, MLA_QK, mqn_ref[...]) * scale
    qmr = _group_rms(qmr, ss_r, MLA_QK, mqr_ref[...])
    qmr = _rope(qmr, _tile_lanes(c32, 2), _tile_lanes(s32, 2), 8) * scale
    q_rope = []
    for hd in range(NH):
        blk = qmr[:, LANES * (hd // 4):LANES * (hd // 4) + LANES]
        up = (HD - MLA_ROPE * (hd % 4)) % LANES
        q_rope.append(_lanes_up(blk, up) if up else blk)
    qm_ref[...] = _mla_head_blocks(qmn, q_rope)

    kr = p[:, 1408:1536]
    kr = kr * lax.rsqrt(jnp.sum(kr * kr, axis=-1, keepdims=True) * (1.0 / MLA_ROPE) + EPS) * krg_ref[...]
    kr = _rope(kr, c32, s32, 8)
    kr_ref[...] = kr

    ckv = p[:, 1152:1408]
    ckv = ckv * lax.rsqrt(jnp.mean(ckv * ckv, axis=-1, keepdims=True) + EPS) * ckvg_ref[...]
    ckv_ref[...] = ckv
    kv = _dot(ckv.astype(BF16), wukv_ref[...])
    kn = kv[:, 0:512]
    kn = _group_rms(kn, _dot((kn * kn).astype(BF16), b64), HD, mkn_ref[...])
    km_ref[...] = _mla_head_blocks(kn, [pltpu.roll(kr, HD, 1)] * NH)
    vm_ref[...] = kv[:, 512:1024].astype(BF16)

    u_ref[...] = p[:, 1536:2048] * jax.nn.sigmoid(p[:, 2048:2560])


def _inproj_call(l, x, mods, lw, consts):
    def full(a):
        return pl.BlockSpec(a.shape, lambda i: (0,) * a.ndim)

    def layer(a):
        return pl.BlockSpec((None,) + a.shape[1:], lambda i: (l,) + (0,) * (a.ndim - 1))

    def rows(w):
        return pl.BlockSpec((TM, w), lambda i: (i, 0))

    def tab():
        return pl.BlockSpec((TM, LANES), lambda i: (_pos_blk(i), 0))

    layer_ws = [lw["norm1_g"], lw["wa"], lw["w_uq"], lw["w_ukv"], lw["gqa_qn_g"], lw["gqa_kn_g"],
                lw["mla_cq_g"], lw["mla_ckv_g"], lw["mla_krn_g"], lw["mla_qn_n"], lw["mla_qn_r"],
                lw["mla_kn_g"]]
    tabs = [consts["c64"], consts["s64"], consts["c32"], consts["s32"]]
    bms = [consts["b64"], consts["brn"], consts["bnr"], consts["brr"]]
    out_w = [(512, BF16), (128, F32), (128, F32), (NH * LANES, BF16), (256, F32), (128, F32),
             (NH * LANES, BF16), (512, BF16), (512, F32)]
    return pl.pallas_call(
        _inproj_kernel,
        grid=(N_TILES,),
        in_specs=[rows(D), pl.BlockSpec((None, 1, 6 * D), lambda i: (l * 8 + _grp(i), 0, 0))]
        + [layer(a) for a in layer_ws] + [tab() for _ in tabs] + [full(a) for a in bms],
        out_specs=[rows(w) for w, _ in out_w],
        out_shape=[jax.ShapeDtypeStruct((T, w), dt) for w, dt in out_w],
        compiler_params=_cparams(("arbitrary",)),
        name="inproj",
    )(x, mods, *layer_ws, *tabs, *bms)


def _cache_kv_kernel(ckv_ref, kr_ref, wukv_ref, mkn_ref, b64_ref, km_ref, vm_ref):
    kv = _dot(ckv_ref[...].astype(BF16), wukv_ref[...])
    kn = kv[:, 0:512]
    kn = _group_rms(kn, _dot((kn * kn).astype(BF16), b64_ref[...]), HD, mkn_ref[...])
    km_ref[...] = _mla_head_blocks(kn, [kr_ref[...]] * NH)
    vm_ref[...] = kv[:, 512:1024].astype(BF16)


def _cache_kv_call(cache_ckv, cache_kr_placed, w_ukv, mla_kn_g, b64):
    def out(w):
        return pl.BlockSpec((None, None, PAST, w), lambda b, l: (b, l, 0, 0))

    return pl.pallas_call(
        _cache_kv_kernel,
        grid=(N_LAT_B, DEPTH),
        in_specs=[
            out(MLA_KV_RANK), out(LANES),
            pl.BlockSpec((None, MLA_KV_RANK, 1024), lambda b, l: (l, 0, 0)),
            pl.BlockSpec((None, 1, 512), lambda b, l: (l, 0, 0)),
            pl.BlockSpec((512, 512), lambda b, l: (0, 0)),
        ],
        out_specs=[out(NH * LANES), out(512)],
        out_shape=[jax.ShapeDtypeStruct((N_LAT_B, DEPTH, PAST, NH * LANES), BF16),
                   jax.ShapeDtypeStruct((N_LAT_B, DEPTH, PAST, 512), BF16)],
        compiler_params=_cparams(("arbitrary", "arbitrary")),
        name="cache_kv",
    )(cache_ckv, cache_kr_placed, w_ukv, mla_kn_g, b64)


def _attn_kernel(*refs, mla, has_cache):
    if has_cache:
        q_ref, ks_ref, vs_ref, kc_ref, vc_ref, o_ref = refs
    else:
        q_ref, ks_ref, vs_ref, o_ref = refs
    qk_w = LANES if mla else HD

    def v_with_ones(v_ref, blk, half):
        v = v_ref[:, LANES * blk:LANES * blk + LANES].astype(BF16)
        lane = lax.broadcasted_iota(I32, v.shape, 1)
        return jnp.where(lane >= HD if half == 1 else lane < HD, v, jnp.ones_like(v))

    halves = []
    for h in range(NH):
        kvh = h if mla else h // GQA_GROUP
        blk, half = kvh // 2, kvh % 2
        sl = slice(qk_w * kvh, qk_w * kvh + qk_w)
        qh = q_ref[:, qk_w * h:qk_w * h + qk_w]
        s = _dot_nt(qh, ks_ref[:, sl].astype(BF16))
        m = jnp.max(s, axis=-1, keepdims=True)
        if has_cache:
            sc = _dot_nt(qh, kc_ref[:, sl].astype(BF16))
            m = jnp.maximum(m, jnp.max(sc, axis=-1, keepdims=True))
        o = _dot(jnp.exp((s - m).astype(BF16)), v_with_ones(vs_ref, blk, half))
        if has_cache:
            o = o + _dot(jnp.exp((sc - m).astype(BF16)), v_with_ones(vc_ref, blk, half))
        den = o[:, 0:1] if half == 1 else o[:, HD:HD + 1]
        on = o / den
        if half != h % 2:
            on = pltpu.roll(on, HD, 1)
        halves.append(on)
    lane = lax.broadcasted_iota(I32, (halves[0].shape[0], LANES), 1)
    o_ref[...] = jnp.concatenate(
        [jnp.where(lane < HD, halves[2 * b], halves[2 * b + 1]).astype(BF16) for b in range(NH // 2)], axis=1)


def _attn_call(l, latent, mla, q, ks, vs, kc, vc):
    if latent:
        nb, nq, n_self, q0, k0 = N_LAT_B, S_LAT // TM, S_LAT, CTX_TILES, T_CTX // S_LAT
    else:
        nb, nq, n_self, q0, k0 = N_CTX_B, 1, S_CTX, 0, 0

    def qspec(w):
        return pl.BlockSpec((TM, w), lambda b, i: (q0 + b * nq + i, 0))

    def kspec(w):
        return pl.BlockSpec((n_self, w), lambda b, i: (k0 + b, 0))

    def cspec(w):
        return pl.BlockSpec((None, None, PAST, w), lambda b, i: (b, l, 0, 0))

    args = [q, ks, vs]
    specs = [qspec(q.shape[1]), kspec(ks.shape[1]), kspec(vs.shape[1])]
    if latent:
        args += [kc, vc]
        specs += [cspec(ks.shape[1]), cspec(vs.shape[1])]
    return pl.pallas_call(
        functools.partial(_attn_kernel, mla=mla, has_cache=latent),
        grid=(nb, nq),
        in_specs=specs,
        out_specs=pl.BlockSpec((TM, 512), lambda b, i: (b * nq + i, 0)),
        out_shape=jax.ShapeDtypeStruct((nb * nq * TM, 512), BF16),
        compiler_params=_cparams(("arbitrary", "arbitrary")),
        name=("mla" if mla else "gqa") + ("_lat" if latent else "_ctx"),
    )(*args)


def _conv_kernel(up_ref, uc_ref, un_ref, dw_ref, dwb_ref, lng_ref, lnb_ref, o_ref, win_ref):
    i = pl.program_id(0)
    in_seq = (i - CTX_TILES) % LAT_TILES_PER_B
    has_prev = jnp.logical_and(i >= CTX_TILES, in_seq != 0)
    has_next = jnp.logical_and(i >= CTX_TILES, in_seq != LAT_TILES_PER_B - 1)
    win_ref[0:16, :] = jnp.where(has_prev, up_ref[TM - 16:TM, :], 0.0)
    win_ref[16:16 + TM, :] = uc_ref[...]
    win_ref[16 + TM:32 + TM, :] = jnp.where(has_next, un_ref[0:16, :], 0.0)
    acc = jnp.zeros((TM, CONV_CH), F32) + dwb_ref[...]
    for j in range(CONV_K):
        acc = acc + win_ref[j + 1:j + 1 + TM, :] * dw_ref[j:j + 1, :]
    mu = jnp.mean(acc, axis=-1, keepdims=True)
    cen = acc - mu
    var = jnp.mean(cen * cen, axis=-1, keepdims=True)
    y = cen * lax.rsqrt(var + EPS) * lng_ref[...] + lnb_ref[...]
    o_ref[...] = (y * jax.nn.sigmoid(y)).astype(BF16)


def _conv_call(l, u, lw):
    def layer(a):
        return pl.BlockSpec((None,) + a.shape[1:], lambda i: (l,) + (0,) * (a.ndim - 1))

    ws = [lw["conv_dw"], lw["conv_dw_b"], lw["conv_ln_g"], lw["conv_ln_b"]]
    return pl.pallas_call(
        _conv_kernel,
        grid=(N_TILES,),
        in_specs=[
            pl.BlockSpec((TM, CONV_CH), lambda i: (jnp.maximum(i - 1, 0), 0)),
            pl.BlockSpec((TM, CONV_CH), lambda i: (i, 0)),
            pl.BlockSpec((TM, CONV_CH), lambda i: (jnp.minimum(i + 1, N_TILES - 1), 0)),
        ] + [layer(a) for a in ws],
        out_specs=pl.BlockSpec((TM, CONV_CH), lambda i: (i, 0)),
        out_shape=jax.ShapeDtypeStruct((T, CONV_CH), BF16),
        scratch_shapes=[pltpu.VMEM((TM + 32, CONV_CH), F32)],
        compiler_params=_cparams(("arbitrary",)),
        name="conv",
    )(u, u, u, *ws)


def _merge_kernel(x_ref, mod_ref, g1_ref, wg_ref, agc_ref, agl_ref, amc_ref, aml_ref, cv_ref,
                  wog_ref, wom_ref, wpw_ref, wout_ref, g2_ref, wr_hi_ref, wr_lo_ref, br_ref,
                  x1_ref, h2_ref, gt_ref, cnt_ref):
    i = pl.program_id(0)
    is_ctx = i < CTX_TILES
    x = x_ref[...]
    mod = mod_ref[...]
    h = _norm_mod(x, g1_ref[...], mod[:, D:2 * D], mod[:, 0:D]).astype(BF16)
    gates = jax.nn.sigmoid(_dot(h, wg_ref[...]))
    ag = jnp.where(is_ctx, agc_ref[...], agl_ref[...])
    am = jnp.where(is_ctx, amc_ref[...], aml_ref[...])
    merged = (gates[:, 0:D] * _dot(ag, wog_ref[...])
              + gates[:, D:2 * D] * _dot(am, wom_ref[...])
              + gates[:, 2 * D:3 * D] * _dot(cv_ref[...], wpw_ref[...]))
    x1 = x + mod[:, 2 * D:3 * D] * _dot(merged.astype(BF16), wout_ref[...])
    x1_ref[...] = x1
    h2 = _norm_mod(x1, g2_ref[...], mod[:, 4 * D:5 * D], mod[:, 3 * D:4 * D])
    h2_hi = h2.astype(BF16)
    h2_ref[...] = h2_hi
    h2_lo = (h2 - h2_hi.astype(F32)).astype(BF16)
    wr_hi = wr_hi_ref[...]
    logit = _dot_nt(wr_hi, h2_hi) + _dot_nt(wr_hi, h2_lo) + _dot_nt(wr_lo_ref[...], h2_hi) + br_ref[...]
    row = lax.broadcasted_iota(I32, logit.shape, 0).astype(F32)
    sels, vals = [], []
    for _ in range(TOP_K):
        m = jnp.max(logit, axis=0, keepdims=True)
        idx = jnp.min(jnp.where(logit == m, row, float(N_EXP)), axis=0, keepdims=True)
        sel = row == idx
        sels.append(sel)
        vals.append(m)
        logit = jnp.where(sel, -jnp.inf, logit)
    es = [jnp.exp(v - vals[0]) for v in vals]
    den = es[0] + es[1] + es[2] + es[3]
    gate = jnp.zeros(logit.shape, F32)
    for sel, e in zip(sels, es):
        gate = jnp.where(sel, e / den, gate)
    gt_ref[...] = gate
    n = jnp.sum((gate > 0.0).astype(F32), axis=1, keepdims=True)
    cnt_ref[...] = jnp.broadcast_to(n, (N_EXP, LANES)).astype(I32)


def _merge_call(l, x, mods, lw, ag_c, ag_l, am_c, am_l, cv):
    def layer(a):
        return pl.BlockSpec((None,) + a.shape[1:], lambda i: (l,) + (0,) * (a.ndim - 1))

    def rows(w):
        return pl.BlockSpec((TM, w), lambda i: (i, 0))

    ctx_rows = pl.BlockSpec((TM, 512), lambda i: (jnp.minimum(i, CTX_TILES - 1), 0))
    lat_rows = pl.BlockSpec((TM, 512), lambda i: (jnp.maximum(i - CTX_TILES, 0), 0))
    ws1 = [lw["norm1_g"], lw["wg"]]
    ws2 = [lw["w_o_gqa"], lw["w_o_mla"], lw["w_pw2"], lw["w_out"], lw["norm2_g"],
           lw["wr_hi"], lw["wr_lo"], lw["b_router"]]
    return pl.pallas_call(
        _merge_kernel,
        grid=(N_TILES,),
        in_specs=[rows(D), pl.BlockSpec((None, 1, 6 * D), lambda i: (l * 8 + _grp(i), 0, 0))]
        + [layer(a) for a in ws1] + [ctx_rows, lat_rows, ctx_rows, lat_rows, rows(512)]
        + [layer(a) for a in ws2],
        out_specs=[rows(D), rows(D), pl.BlockSpec((N_EXP, TM), lambda i: (0, i)),
                   pl.BlockSpec((None, N_EXP, LANES), lambda i: (i, 0, 0))],
        out_shape=[jax.ShapeDtypeStruct((T, D), F32), jax.ShapeDtypeStruct((T, D), BF16),
                   jax.ShapeDtypeStruct((N_EXP, T), F32),
                   jax.ShapeDtypeStruct((N_TILES, N_EXP, LANES), I32)],
        compiler_params=_cparams(("arbitrary",)),
        name="merge",
    )(x, mods, *ws1, ag_c, ag_l, am_c, am_l, cv, *ws2)


def _moe_tables(cnt):
    n8 = (cnt + SUB - 1) // SUB * SUB
    loc = jnp.cumsum(n8, axis=1) - n8
    tot = jnp.sum(n8, axis=0)
    ntile = (tot + PIECE - SUB + ET - 1) // ET
    seg = ntile * ET + SEG_SLACK
    off = jnp.cumsum(seg) - seg
    start = off[None, :] + jnp.cumsum(n8, axis=0) - n8
    npc = (n8 + PIECE - 1) // PIECE
    pc = jnp.cumsum(npc, axis=1)
    p = jnp.arange(N_PIECE_MAX, dtype=I32)[None, :, None]
    owner = jnp.logical_and((pc - npc)[:, None, :] <= p, p < pc[:, None, :])

    def of_owner(a):
        return jnp.sum(jnp.where(owner, a[:, None, :], 0), axis=2)

    k = p[:, :, 0] - of_owner(pc - npc)
    live = jnp.any(owner, axis=2)
    src = of_owner(loc) + PIECE * k
    dst = of_owner(start) + PIECE * k
    loc_c = (pc - npc) * PIECE
    src_c = of_owner(loc_c) + PIECE * k

    def col_table(a):
        return jnp.broadcast_to(a.astype(F32)[:, :, None], (N_TILES, N_EXP, LANES))

    def row_table(a):
        a = jnp.concatenate([a.astype(F32), jnp.zeros((N_TILES, LANES - N_EXP), F32)], axis=1)
        return jnp.broadcast_to(a[:, None, :], (N_TILES, SUB, LANES))

    return dict(src=jnp.where(live, src, 0).reshape(-1).astype(I32),
                src_c=jnp.where(live, src_c, 0).reshape(-1).astype(I32),
                dst=jnp.where(live, dst, 0).reshape(-1).astype(I32),
                n_piece=pc[:, -1].astype(I32), flush=(off + tot).astype(I32),
                first=(off // ET).astype(I32), ntile=ntile.astype(I32),
                lo_row=row_table(loc), hi_row=row_table(loc + n8),
                lo_col=col_table(loc_c), hi_col=col_table(loc_c + n8))


def _rank_in_tile(gate_t):
    sel = gate_t > 0.0
    r = lax.broadcasted_iota(I32, (TM, TM), 0)
    c = lax.broadcasted_iota(I32, (TM, TM), 1)
    upper = (r < c).astype(BF16)
    rank = _dot(sel.astype(BF16), upper)
    return jnp.where(sel, rank, -1e4)


def _piece_loop(n, body):
    def step(p, carry):
        body(p)
        return carry

    lax.fori_loop(0, n, step, 0)


def _dispatch_kernel(src_ref, dst_ref, npc_ref, flush_ref, gt_ref, lo_ref, hi_ref, h2_ref, xs_ref,
                     buf_ref, zero_ref, sem_ref, zsem_ref):
    i = pl.program_id(0)
    slot = i & 1

    def piece_copy(tile, p, s):
        src = src_ref[tile * N_PIECE_MAX + p]
        dst = dst_ref[tile * N_PIECE_MAX + p]
        return pltpu.make_async_copy(buf_ref.at[s, pl.ds(pl.multiple_of(src, SUB), PIECE)],
                                     xs_ref.at[pl.ds(pl.multiple_of(dst, SUB), PIECE)], sem_ref.at[s])

    @pl.when(i == 0)
    def _():
        zero_ref[...] = jnp.zeros(zero_ref.shape, F32)
        copies = [pltpu.make_async_copy(
            zero_ref, xs_ref.at[pl.ds(pl.multiple_of(flush_ref[e], SUB), FLUSH)], zsem_ref.at[0])
            for e in range(N_EXP)]
        for cp in copies:
            cp.start()
        for cp in copies:
            cp.wait()

    gate_t = gt_ref[...]
    rk = _rank_in_tile(gate_t).astype(BF16)
    g_hi, g_mid, g_lo = _split3(gate_t)
    h2 = h2_ref[...]
    lo = lo_ref[0:1, 0:N_EXP]
    hi = hi_ref[0:1, 0:N_EXP]
    for c in range(SORT_CHUNKS):
        r = (lax.broadcasted_iota(I32, (TM, N_EXP), 0) + c * TM).astype(F32)
        own = jnp.logical_and(r >= lo, r < hi)
        base = jnp.sum(jnp.where(own, lo, 0.0), axis=1, keepdims=True)
        owned = jnp.sum(jnp.where(own, 1.0, 0.0), axis=1, keepdims=True) > 0.0
        in_run = jnp.where(owned, r[:, 0:1] - base, -1.0)
        expand = own.astype(BF16)
        hit = _dot(expand, rk) == in_run
        gexp = _dot(expand, g_hi) + _dot(expand, g_mid) + _dot(expand, g_lo)
        w = jnp.sum(jnp.where(hit, gexp, 0.0), axis=1, keepdims=True)
        buf_ref[slot, c * TM:(c + 1) * TM, 0:D] = _dot(hit.astype(BF16), h2)
        buf_ref[slot, c * TM:(c + 1) * TM, D:XS_W] = jnp.broadcast_to(w, (TM, LANES))

    @pl.when(i > 0)
    def _():
        _piece_loop(npc_ref[i - 1], lambda p: piece_copy(i - 1, p, 1 - slot).wait())

    _piece_loop(npc_ref[i], lambda p: piece_copy(i, p, slot).start())

    @pl.when(i == pl.num_programs(0) - 1)
    def _():
        _piece_loop(npc_ref[i], lambda p: piece_copy(i, p, slot).wait())


def _dispatch_call(tb, gate_t, h2):
    tab = pl.BlockSpec((None, SUB, LANES), lambda i, *_: (i, 0, 0))
    return pl.pallas_call(
        _dispatch_kernel,
        grid_spec=pltpu.PrefetchScalarGridSpec(
            num_scalar_prefetch=4,
            grid=(N_TILES,),
            in_specs=[pl.BlockSpec((N_EXP, TM), lambda i, *_: (0, i)), tab, tab,
                      pl.BlockSpec((TM, D), lambda i, *_: (i, 0))],
            out_specs=pl.BlockSpec(memory_space=pl.ANY),
            scratch_shapes=[pltpu.VMEM((2, SORT_ROWS, XS_W), F32), pltpu.VMEM((FLUSH, XS_W), F32),
                            pltpu.SemaphoreType.DMA((2,)), pltpu.SemaphoreType.DMA((1,))]),
        out_shape=jax.ShapeDtypeStruct((N_SLOTS, XS_W), F32),
        compiler_params=_cparams(("arbitrary",)),
        name="moe_dispatch",
    )(tb["src"], tb["dst"], tb["n_piece"], tb["flush"], gate_t, tb["lo_row"], tb["hi_row"], h2)


def _expert_kernel(first_ref, ntile_ref, bgu_ref, bdn_ref, xs_ref, wgu_ref, wdn_ref, ys_ref,
                   wgu_f32_ref, wdn_f32_ref, wgu_bf_ref, wdn_bf_ref, x_buf_ref, y_buf_ref,
                   wsem_ref, xsem_ref, ysem_ref, *, layer):
    e = pl.program_id(0)
    slot = e & 1

    def weight_copies(expert, s):
        return (pltpu.make_async_copy(wgu_ref.at[layer, expert], wgu_f32_ref.at[s], wsem_ref.at[0, s]),
                pltpu.make_async_copy(wdn_ref.at[layer, expert], wdn_f32_ref.at[s], wsem_ref.at[1, s]))

    def x_copy(t, s):
        row = pl.multiple_of((first_ref[e] + t) * ET, ET)
        return pltpu.make_async_copy(xs_ref.at[pl.ds(row, ET)], x_buf_ref.at[s], xsem_ref.at[s])

    def y_copy(t, s):
        row = pl.multiple_of((first_ref[e] + t) * ET, ET)
        return pltpu.make_async_copy(y_buf_ref.at[s], ys_ref.at[pl.ds(row, ET)], ysem_ref.at[s])

    @pl.when(e == 0)
    def _():
        for cp in weight_copies(0, 0):
            cp.start(priority=1)

    @pl.when(e + 1 < pl.num_programs(0))
    def _():
        for cp in weight_copies(e + 1, 1 - slot):
            cp.start(priority=1)

    n_t = ntile_ref[e]

    @pl.when(n_t > 0)
    def _():
        x_copy(0, 0).start()

    for cp in weight_copies(e, slot):
        cp.wait()
    for r in range(0, D, 128):
        wgu_bf_ref[r:r + 128, :] = wgu_f32_ref[slot, r:r + 128, :].astype(BF16)
    for r in range(0, FF, 128):
        wdn_bf_ref[r:r + 128, :] = wdn_f32_ref[slot, r:r + 128, :].astype(BF16)
    bgu = bgu_ref[...]
    bdn = bdn_ref[...]

    def row_tile(t, carry):
        s = t & 1

        @pl.when(t + 1 < n_t)
        def _():
            x_copy(t + 1, 1 - s).start()

        x_copy(t, s).wait()

        @pl.when(t >= 2)
        def _():
            y_copy(t - 2, s).wait()

        x = x_buf_ref[s, :, 0:D].astype(BF16)
        w = x_buf_ref[s, :, D:D + 1]
        gu = _dot(x, wgu_bf_ref[...]) + bgu
        gl = jnp.minimum(gu[:, 0:FF], SWIGLU_LIMIT)
        up = jnp.clip(gu[:, FF:2 * FF], -SWIGLU_LIMIT, SWIGLU_LIMIT)
        act = (up + 1.0) * (gl * jax.nn.sigmoid(SWIGLU_ALPHA * gl))
        y_buf_ref[s] = w * (_dot(act.astype(BF16), wdn_bf_ref[...]) + bdn)
        y_copy(t, s).start()
        return carry

    lax.fori_loop(0, n_t, row_tile, 0)

    @pl.when(n_t >= 2)
    def _():
        y_copy(n_t - 2, n_t & 1).wait()

    @pl.when(n_t >= 1)
    def _():
        y_copy(n_t - 1, (n_t - 1) & 1).wait()


def _expert_call(l, tb, xs, w_gu, b_gu, w_dn, b_dn):
    return pl.pallas_call(
        functools.partial(_expert_kernel, layer=l),
        grid_spec=pltpu.PrefetchScalarGridSpec(
            num_scalar_prefetch=2,
            grid=(N_EXP,),
            in_specs=[
                pl.BlockSpec((None, None, 1, 2 * FF), lambda e, *_: (l, e, 0, 0)),
                pl.BlockSpec((None, None, 1, D), lambda e, *_: (l, e, 0, 0)),
                pl.BlockSpec(memory_space=pl.ANY),
                pl.BlockSpec(memory_space=pl.ANY),
                pl.BlockSpec(memory_space=pl.ANY),
            ],
            out_specs=pl.BlockSpec(memory_space=pl.ANY),
            scratch_shapes=[pltpu.VMEM((2, D, 2 * FF), F32), pltpu.VMEM((2, FF, D), F32),
                            pltpu.VMEM((D, 2 * FF), BF16), pltpu.VMEM((FF, D), BF16),
                            pltpu.VMEM((2, ET, XS_W), F32), pltpu.VMEM((2, ET, D), F32),
                            pltpu.SemaphoreType.DMA((2, 2)), pltpu.SemaphoreType.DMA((2,)),
                            pltpu.SemaphoreType.DMA((2,))]),
        out_shape=jax.ShapeDtypeStruct((N_SLOTS, D), F32),
        compiler_params=_cparams(("arbitrary",)),
        name="moe_experts",
    )(tb["first"], tb["ntile"], b_gu.reshape(DEPTH, N_EXP, 1, 2 * FF), b_dn.reshape(DEPTH, N_EXP, 1, D),
      xs, w_gu, w_dn)


def _combine_kernel(src_ref, dst_ref, npc_ref, gt_ref, lo_ref, hi_ref, x1_ref, mod_ref, ys_ref, o_ref,
                    buf_ref, sem_ref):
    i = pl.program_id(0)
    slot = i & 1

    def piece_copy(tile, p, s):
        src = src_ref[tile * N_PIECE_MAX + p]
        dst = dst_ref[tile * N_PIECE_MAX + p]
        return pltpu.make_async_copy(ys_ref.at[pl.ds(pl.multiple_of(dst, SUB), PIECE)],
                                     buf_ref.at[s, pl.ds(pl.multiple_of(src, SUB), PIECE)], sem_ref.at[s])

    @pl.when(i == 0)
    def _():
        buf_ref[...] = jnp.zeros(buf_ref.shape, F32)
        _piece_loop(npc_ref[0], lambda p: piece_copy(0, p, 0).start())

    @pl.when(i + 1 < pl.num_programs(0))
    def _():
        _piece_loop(npc_ref[i + 1], lambda p: piece_copy(i + 1, p, 1 - slot).start())

    rank_t = _rank_in_tile(gt_ref[...])
    r = lax.broadcasted_iota(I32, (TM, TM), 0)
    c = lax.broadcasted_iota(I32, (TM, TM), 1)
    eye = (r == c).astype(BF16)
    rk = _dot_nt(eye, rank_t.astype(BF16)).astype(BF16)
    lo = lo_ref[:, 0:1]
    hi = hi_ref[:, 0:1]

    _piece_loop(npc_ref[i], lambda p: piece_copy(i, p, slot).wait())

    def chunk(cidx, acc):
        rr = (lax.broadcasted_iota(I32, (N_EXP, TM), 1) + cidx * TM).astype(F32)
        own = jnp.logical_and(rr >= lo, rr < hi)
        base = jnp.sum(jnp.where(own, lo, 0.0), axis=0, keepdims=True)
        owned = jnp.sum(jnp.where(own, 1.0, 0.0), axis=0, keepdims=True) > 0.0
        in_run = jnp.where(owned, rr[0:1, :] - base, -1.0)
        hit = _dot(rk, own.astype(BF16)) == in_run
        y = buf_ref[slot, pl.ds(pl.multiple_of(cidx * TM, TM), TM), :].astype(BF16)
        return acc + _dot(hit.astype(BF16), y)

    n_chunk = (npc_ref[i] * PIECE + TM - 1) // TM
    acc = lax.fori_loop(0, n_chunk, chunk, jnp.zeros((TM, D), F32))
    o_ref[...] = x1_ref[...] + mod_ref[...][:, 5 * D:6 * D] * acc


def _combine_call(l, tb, gate_t, x1, mods, ys):
    tab = pl.BlockSpec((None, N_EXP, LANES), lambda i, *_: (i, 0, 0))
    return pl.pallas_call(
        _combine_kernel,
        grid_spec=pltpu.PrefetchScalarGridSpec(
            num_scalar_prefetch=3,
            grid=(N_TILES,),
            in_specs=[
                pl.BlockSpec((N_EXP, TM), lambda i, *_: (0, i)), tab, tab,
                pl.BlockSpec((TM, D), lambda i, *_: (i, 0)),
                pl.BlockSpec((None, 1, 6 * D), lambda i, *_: (l * 8 + _grp(i), 0, 0)),
                pl.BlockSpec(memory_space=pl.ANY),
            ],
            out_specs=pl.BlockSpec((TM, D), lambda i, *_: (i, 0)),
            scratch_shapes=[pltpu.VMEM((2, UNSORT_ROWS, D), F32), pltpu.SemaphoreType.DMA((2,))]),
        out_shape=jax.ShapeDtypeStruct((T, D), F32),
        compiler_params=_cparams(("arbitrary",)),
        name="moe_combine",
    )(tb["src_c"], tb["dst"], tb["n_piece"], gate_t, tb["lo_col"], tb["hi_col"], x1, mods, ys)


def _rope_tables():
    t = jnp.arange(S_LAT, dtype=I32)
    row = (t // GRID_W).astype(F32)
    col = (t % GRID_W).astype(F32)

    def table(h, reps):
        inv = ROPE_THETA ** (-jnp.arange(0, h, 2, dtype=F32) / h)
        ar = row[:, None] * inv[None, :]
        ac = col[:, None] * inv[None, :]
        cos = jnp.concatenate([jnp.cos(ar), jnp.cos(ar), jnp.cos(ac), jnp.cos(ac)], axis=-1)
        sin = jnp.concatenate([-jnp.sin(ar), jnp.sin(ar), -jnp.sin(ac), jnp.sin(ac)], axis=-1)
        cos = jnp.tile(cos, (1, reps))
        sin = jnp.tile(sin, (1, reps))
        ident_c = jnp.ones((TM, LANES), F32)
        ident_s = jnp.zeros((TM, LANES), F32)
        return jnp.concatenate([ident_c, cos], axis=0), jnp.concatenate([ident_s, sin], axis=0)

    c64, s64 = table(HD // 2, 2)
    c32, s32 = table(MLA_ROPE // 2, 4)
    return c64, s64, c32, s32


def _group_matrix(n_rows, row_group, n_cols, col_group):
    r = np.arange(n_rows)[:, None] // row_group
    c = np.arange(n_cols)[None, :] // col_group
    return jnp.asarray((r == c).astype(np.float32), dtype=BF16)


def kernel(x_prompt, x_sample, cache_gqa_k, cache_gqa_v, cache_mla_ckv, cache_mla_krope, c, c_ctx,
           w_ada, b_ada, norm1_g, w_in, gqa_qn_g, gqa_kn_g, w_o_gqa, mla_cq_g, w_uq, mla_ckv_g, w_ukv,
           mla_qn_g, mla_kn_g, mla_krn_g, w_o_mla, conv_dw, conv_dw_b, conv_ln_g, conv_ln_b, w_pw2,
           w_out, norm2_g, w_router, b_router, w_gu, b_gu, w_dn, b_dn):
    L = DEPTH
    c64, s64, c32, s32 = _rope_tables()
    consts = dict(c64=c64, s64=s64, c32=c32, s32=s32,
                  b64=_group_matrix(512, HD, 512, HD), brn=_group_matrix(256, MLA_ROPE, 512, HD),
                  bnr=_group_matrix(512, HD, 256, MLA_ROPE),
                  brr=_group_matrix(256, MLA_ROPE, 256, MLA_ROPE))

    pad = jnp.zeros((L, D, LANES - MLA_ROPE), F32)
    wa = jnp.concatenate([w_in[:, :, 0:1440], pad, w_in[:, :, 1440:2464]], axis=-1).astype(BF16)
    wg = w_in[:, :, 2464:].astype(BF16)
    uq = w_uq.reshape(L, MLA_Q_RANK, NH, MLA_QK)
    uq = jnp.concatenate([uq[..., :HD].reshape(L, MLA_Q_RANK, NH * HD),
                          uq[..., HD:].reshape(L, MLA_Q_RANK, NH * MLA_ROPE)], axis=-1).astype(BF16)
    ukv = w_ukv.reshape(L, MLA_KV_RANK, NH, 2 * HD)
    ukv = jnp.concatenate([ukv[..., :HD].reshape(L, MLA_KV_RANK, NH * HD),
                           ukv[..., HD:].reshape(L, MLA_KV_RANK, NH * HD)], axis=-1).astype(BF16)

    def row3(a):
        return a.reshape(L, 1, a.shape[-1])

    wr_t = jnp.swapaxes(w_router, 1, 2)
    wr_hi = wr_t.astype(BF16)
    lw = dict(
        norm1_g=row3(norm1_g), wa=wa, wg=wg, w_uq=uq, w_ukv=ukv,
        gqa_qn_g=row3(jnp.tile(gqa_qn_g, (1, NH))), gqa_kn_g=row3(jnp.tile(gqa_kn_g, (1, GQA_KV))),
        mla_cq_g=row3(mla_cq_g), mla_ckv_g=row3(mla_ckv_g),
        mla_krn_g=row3(jnp.concatenate([mla_krn_g, jnp.zeros((L, LANES - MLA_ROPE), F32)], axis=-1)),
        mla_qn_n=row3(jnp.tile(mla_qn_g[:, :HD], (1, NH))),
        mla_qn_r=row3(jnp.tile(mla_qn_g[:, HD:], (1, NH))),
        mla_kn_g=row3(jnp.tile(mla_kn_g, (1, NH))),
        conv_dw=conv_dw, conv_dw_b=row3(conv_dw_b), conv_ln_g=row3(conv_ln_g), conv_ln_b=row3(conv_ln_b),
        w_o_gqa=w_o_gqa.astype(BF16), w_o_mla=w_o_mla.astype(BF16), w_pw2=w_pw2.astype(BF16),
        w_out=w_out.astype(BF16), norm2_g=row3(norm2_g),
        wr_hi=wr_hi, wr_lo=(wr_t - wr_hi.astype(F32)).astype(BF16),
        b_router=b_router.reshape(L, N_EXP, 1),
    )

    cvec = jnp.concatenate([c_ctx[None, :], c, jnp.zeros((8 - 1 - N_LAT_B, D), F32)], axis=0)
    mods = _ada_call(cvec, w_ada, b_ada).reshape(L * 8, 1, 6 * D)

    ck = cache_gqa_k.reshape(N_LAT_B, L, PAST, GQA_KV * HD)
    cv_ = cache_gqa_v.reshape(N_LAT_B, L, PAST, GQA_KV * HD)
    kr_placed = jnp.pad(cache_mla_krope, ((0, 0), (0, 0), (0, 0), (HD, LANES - MLA_QK)))
    km_c, vm_c = _cache_kv_call(cache_mla_ckv, kr_placed, ukv, lw["mla_kn_g"], consts["b64"])

    x = jnp.concatenate([x_prompt.reshape(T_CTX, D), x_sample.reshape(T_LAT, D)], axis=0)
    new_k, new_v, new_ckv, new_kr = [], [], [], []
    for l in range(L):
        qg, kg, vg, qm, ckv, kr, km, vm, u = _inproj_call(l, x, mods, lw, consts)
        new_k.append(kg[:T_CTX].reshape(N_CTX_B, S_CTX, GQA_KV, HD))
        new_v.append(vg[:T_CTX].reshape(N_CTX_B, S_CTX, GQA_KV, HD))
        new_ckv.append(ckv[:T_CTX].reshape(N_CTX_B, S_CTX, MLA_KV_RANK))
        new_kr.append(kr[:T_CTX, :MLA_ROPE].reshape(N_CTX_B, S_CTX, MLA_ROPE))
        ag_c = _attn_call(l, False, False, qg, kg, vg, None, None)
        ag_l = _attn_call(l, True, False, qg, kg, vg, ck, cv_)
        am_c = _attn_call(l, False, True, qm, km, vm, None, None)
        am_l = _attn_call(l, True, True, qm, km, vm, km_c, vm_c)
        cv = _conv_call(l, u, lw)
        x1, h2, gate_t, cnt = _merge_call(l, x, mods, lw, ag_c, ag_l, am_c, am_l, cv)
        tb = _moe_tables(cnt[:, :, 0])
        xs = _dispatch_call(tb, gate_t, h2)
        ys = _expert_call(l, tb, xs, w_gu, b_gu, w_dn, b_dn)
        x = _combine_call(l, tb, gate_t, x1, mods, ys)

    return (x[:T_CTX].reshape(N_CTX_B, S_CTX, D), x[T_CTX:].reshape(N_LAT_B, S_LAT, D),
            jnp.stack(new_k, axis=1), jnp.stack(new_v, axis=1),
            jnp.stack(new_ckv, axis=1), jnp.stack(new_kr, axis=1))
```

```python
import functools

import numpy as np
import jax
import jax.numpy as jnp
from jax import lax
from jax.experimental import pallas as pl
from jax.experimental.pallas import tpu as pltpu

F32 = jnp.float32
BF16 = jnp.bfloat16
I32 = jnp.int32

D = 1024
DEPTH = 4
N_CTX_B, S_CTX = 16, 256
N_LAT_B, S_LAT = 2, 2048
PAST = 256
GRID_W = 64
ROPE_THETA = 10000.0
EPS = 1e-6
HD = 64
NH = 8
GQA_KV = 2
GQA_GROUP = NH // GQA_KV
MLA_Q_RANK = 384
MLA_KV_RANK = 256
MLA_ROPE = 32
MLA_QK = HD + MLA_ROPE
CONV_CH = 512
CONV_K = 31
CONV_PAD = CONV_K // 2
N_EXP = 32
TOP_K = 4
FF = 1024
SWIGLU_LIMIT = 7.0
SWIGLU_ALPHA = 1.702

T_CTX = N_CTX_B * S_CTX
T_LAT = N_LAT_B * S_LAT
T = T_CTX + T_LAT

LANES = 128
TM = 256
N_TILES = T // TM
CTX_TILES = T_CTX // TM
LAT_TILES_PER_B = S_LAT // TM
WA = 2560
GATES_W = 3 * D
SUB = 8
PIECE = 32
ET = 256
XS_W = D + LANES
SORT_ROWS = 1280
SORT_CHUNKS = SORT_ROWS // TM
assert TM * TOP_K + (SUB - 1) * N_EXP + PIECE - SUB <= SORT_ROWS
UNSORT_ROWS = 2048
assert TM * TOP_K + (PIECE - 1) * N_EXP <= UNSORT_ROWS
N_PIECE_MAX = (TM * TOP_K + (SUB - 1) * N_EXP) // PIECE + N_EXP
FLUSH = ET + PIECE
SEG_SLACK = 2 * ET
N_ET_MAX = (T * TOP_K + (SUB - 1) * N_TILES * N_EXP + N_EXP * (PIECE - SUB)) // ET + N_EXP
N_SLOTS = N_ET_MAX * ET + N_EXP * SEG_SLACK
VMEM_LIMIT = 56 * 1024 * 1024


def _dot(a, b):
    return jnp.dot(a, b, preferred_element_type=F32)


def _dot_nt(a, b):
    return lax.dot_general(a, b, (((1,), (1,)), ((), ())), preferred_element_type=F32)


def _split3(x):
    hi = x.astype(BF16)
    r = x - hi.astype(F32)
    mid = r.astype(BF16)
    lo = (r - mid.astype(F32)).astype(BF16)
    return hi, mid, lo


def _cparams(sem, vmem=VMEM_LIMIT):
    return pltpu.CompilerParams(dimension_semantics=sem, vmem_limit_bytes=vmem)


def _grp(i):
    return jnp.where(i < CTX_TILES, 0, 1 + (i - CTX_TILES) // LAT_TILES_PER_B)


def _pos_blk(i):
    return jnp.where(i < CTX_TILES, 0, 1 + (i - CTX_TILES) % LAT_TILES_PER_B)


def _ada_kernel(c_ref, w_ref, b_ref, o_ref):
    c = c_ref[...]
    s = c * jax.nn.sigmoid(c)
    w = w_ref[...]
    s_hi, s_lo, _ = _split3(s)
    w_hi, w_lo, _ = _split3(w)
    o_ref[...] = _dot(s_hi, w_hi) + _dot(s_hi, w_lo) + _dot(s_lo, w_hi) + b_ref[...]


def _ada_call(cvec, w_ada, b_ada):
    tn = 1536
    return pl.pallas_call(
        _ada_kernel,
        grid=(DEPTH, 6 * D // tn),
        in_specs=[
            pl.BlockSpec((8, D), lambda l, j: (0, 0)),
            pl.BlockSpec((None, D, tn), lambda l, j: (l, 0, j)),
            pl.BlockSpec((None, 1, tn), lambda l, j: (l, 0, j)),
        ],
        out_specs=pl.BlockSpec((None, 8, tn), lambda l, j: (l, 0, j)),
        out_shape=jax.ShapeDtypeStruct((DEPTH, 8, 6 * D), F32),
        compiler_params=_cparams(("arbitrary", "arbitrary")),
        name="ada_mod",
    )(cvec, w_ada, b_ada.reshape(DEPTH, 1, 6 * D))


def _norm_mod(x, g, scale, shift):
    ms = jnp.mean(x * x, axis=-1, keepdims=True)
    return (x * lax.rsqrt(ms + EPS) * g) * (1.0 + scale) + shift


def _group_rms(x, ss, n, g):
    return x * lax.rsqrt(ss * (1.0 / n) + EPS) * g


def _partner(x, dist):
    n = x.shape[-1]
    lane = lax.broadcasted_iota(I32, x.shape, 1)
    r1 = pltpu.roll(x, dist, 1)
    r2 = pltpu.roll(x, n - dist, 1)
    r1_is_minus = pltpu.roll(lane, dist, 1) == ((lane + (n - dist)) & (n - 1))
    plus = jnp.where(r1_is_minus, r2, r1)
    minus = jnp.where(r1_is_minus, r1, r2)
    return jnp.where((lane & (2 * dist - 1)) < dist, plus, minus)


def _rope(x, cos, sin_signed, dist):
    return x * cos + _partner(x, dist) * sin_signed


def _tile_lanes(x, reps):
    return jnp.concatenate([x] * reps, axis=1)


def _lanes_up(x, s):
    lane = lax.broadcasted_iota(I32, x.shape, 1)
    r1 = pltpu.roll(x, s, 1)
    r2 = pltpu.roll(x, LANES - s, 1)
    r1_moves_up = pltpu.roll(lane, s, 1) == ((lane + (LANES - s)) & (LANES - 1))
    return jnp.where(r1_moves_up, r1, r2)


def _mla_head_blocks(nope, rope_blocks):
    lane = lax.broadcasted_iota(I32, (nope.shape[0], LANES), 1)
    blocks = []
    for h in range(NH):
        pair = nope[:, LANES * (h // 2):LANES * (h // 2) + LANES]
        if h % 2 == 1:
            pair = pltpu.roll(pair, HD, 1)
        blocks.append(jnp.where(lane < HD, pair, jnp.where(lane < MLA_QK, rope_blocks[h], 0.0)))
    return jnp.concatenate(blocks, axis=1).astype(BF16)


def _inproj_kernel(x_ref, mod_ref, g1_ref, wa_ref, wuq_ref, wukv_ref,
                   qng_ref, kng_ref, cqg_ref, ckvg_ref, krg_ref, mqn_ref, mqr_ref, mkn_ref,
                   c64_ref, s64_ref, c32_ref, s32_ref,
                   b64_ref, brn_ref, bnr_ref, brr_ref,
                   qg_ref, kg_ref, vg_ref, qm_ref, ckv_ref, kr_ref, km_ref, vm_ref, u_ref):
    mod = mod_ref[...]
    h = _norm_mod(x_ref[...], g1_ref[...], mod[:, D:2 * D], mod[:, 0:D])
    p = _dot(h.astype(BF16), wa_ref[...])
    b64 = b64_ref[...]
    c64 = c64_ref[...]
    s64 = s64_ref[...]
    c32 = c32_ref[...]
    s32 = s32_ref[...]

    q = p[:, 0:512]
    q = _group_rms(q, _dot((q * q).astype(BF16), b64), HD, qng_ref[...])
    q = _rope(q, _tile_lanes(c64, 4), _tile_lanes(s64, 4), 16)
    qg_ref[...] = (q * (HD ** -0.5)).astype(BF16)
    k = p[:, 512:640]
    k = _group_rms(k, _dot((k * k).astype(BF16), b64[0:128, 0:128]), HD, kng_ref[...])
    kg_ref[...] = _rope(k, c64, s64, 16)
    vg_ref[...] = p[:, 640:768]

    cq = p[:, 768:1152]
    cq = cq * lax.rsqrt(jnp.mean(cq * cq, axis=-1, keepdims=True) + EPS) * cqg_ref[...]
    qm = _dot(cq.astype(BF16), wuq_ref[...])
    qmn = qm[:, 0:512]
    qmr = qm[:, 512:768]
    sqn = (qmn * qmn).astype(BF16)
    sqr = (qmr * qmr).astype(BF16)
    ss_n = _dot(sqn, b64) + _dot(sqr, brn_ref[...])
    ss_r = _dot(sqn, bnr_ref[...]) + _dot(sqr, brr_ref[...])
    scale = MLA_QK ** -0.5
    qmn = _group_rms(qmn, ss_n, MLA_QK, mqn_ref[...]) * scale
    qmr = _group_rms(qmr, ss_r, MLA_QK, mqr_ref[...])
    qmr = _rope(qmr, _tile_lanes(c32, 2), _tile_lanes(s32, 2), 8) * scale
    q_rope = []
    for hd in range(NH):
        blk = qmr[:, LANES * (hd // 4):LANES * (hd // 4) + LANES]
        up = (HD - MLA_ROPE * (hd % 4)) % LANES
        q_rope.append(_lanes_up(blk, up) if up else blk)
    qm_ref[...] = _mla_head_blocks(qmn, q_rope)

    kr = p[:, 1408:1536]
    kr = kr * lax.rsqrt(jnp.sum(kr * kr, axis=-1, keepdims=True) * (1.0 / MLA_ROPE) + EPS) * krg_ref[...]
    kr = _rope(kr, c32, s32, 8)
    kr_ref[...] = kr

    ckv = p[:, 1152:1408]
    ckv = ckv * lax.rsqrt(jnp.mean(ckv * ckv, axis=-1, keepdims=True) + EPS) * ckvg_ref[...]
    ckv_ref[...] = ckv
    kv = _dot(ckv.astype(BF16), wukv_ref[...])
    kn = kv[:, 0:512]
    kn = _group_rms(kn, _dot((kn * kn).astype(BF16), b64), HD, mkn_ref[...])
    km_ref[...] = _mla_head_blocks(kn, [pltpu.roll(kr, HD, 1)] * NH)
    vm_ref[...] = kv[:, 512:1024].astype(BF16)

    u_ref[...] = p[:, 1536:2048] * jax.nn.sigmoid(p[:, 2048:2560])


def _inproj_call(l, x, mods, lw, consts):
    def full(a):
        return pl.BlockSpec(a.shape, lambda i: (0,) * a.ndim)

    def layer(a):
        return pl.BlockSpec((None,) + a.shape[1:], lambda i: (l,) + (0,) * (a.ndim - 1))

    def rows(w):
        return pl.BlockSpec((TM, w), lambda i: (i, 0))

    def tab():
        return pl.BlockSpec((TM, LANES), lambda i: (_pos_blk(i), 0))

    layer_ws = [lw["norm1_g"], lw["wa"], lw["w_uq"], lw["w_ukv"], lw["gqa_qn_g"], lw["gqa_kn_g"],
                lw["mla_cq_g"], lw["mla_ckv_g"], lw["mla_krn_g"], lw["mla_qn_n"], lw["mla_qn_r"],
                lw["mla_kn_g"]]
    tabs = [consts["c64"], consts["s64"], consts["c32"], consts["s32"]]
    bms = [consts["b64"], consts["brn"], consts["bnr"], consts["brr"]]
    out_w = [(512, BF16), (128, F32), (128, F32), (NH * LANES, BF16), (256, F32), (128, F32),
             (NH * LANES, BF16), (512, BF16), (512, F32)]
    return pl.pallas_call(
        _inproj_kernel,
        grid=(N_TILES,),
        in_specs=[rows(D), pl.BlockSpec((None, 1, 6 * D), lambda i: (l * 8 + _grp(i), 0, 0))]
        + [layer(a) for a in layer_ws] + [tab() for _ in tabs] + [full(a) for a in bms],
        out_specs=[rows(w) for w, _ in out_w],
        out_shape=[jax.ShapeDtypeStruct((T, w), dt) for w, dt in out_w],
        compiler_params=_cparams(("arbitrary",)),
        name="inproj",
    )(x, mods, *layer_ws, *tabs, *bms)


def _cache_kv_kernel(ckv_ref, kr_ref, wukv_ref, mkn_ref, b64_ref, km_ref, vm_ref):
    kv = _dot(ckv_ref[...].astype(BF16), wukv_ref[...])
    kn = kv[:, 0:512]
    kn = _group_rms(kn, _dot((kn * kn).astype(BF16), b64_ref[...]), HD, mkn_ref[...])
    km_ref[...] = _mla_head_blocks(kn, [kr_ref[...]] * NH)
    vm_ref[...] = kv[:, 512:1024].astype(BF16)


def _cache_kv_call(cache_ckv, cache_kr_placed, w_ukv, mla_kn_g, b64):
    def out(w):
        return pl.BlockSpec((None, None, PAST, w), lambda b, l: (b, l, 0, 0))

    return pl.pallas_call(
        _cache_kv_kernel,
        grid=(N_LAT_B, DEPTH),
        in_specs=[
            out(MLA_KV_RANK), out(LANES),
            pl.BlockSpec((None, MLA_KV_RANK, 1024), lambda b, l: (l, 0, 0)),
            pl.BlockSpec((None, 1, 512), lambda b, l: (l, 0, 0)),
            pl.BlockSpec((512, 512), lambda b, l: (0, 0)),
        ],
        out_specs=[out(NH * LANES), out(512)],
        out_shape=[jax.ShapeDtypeStruct((N_LAT_B, DEPTH, PAST, NH * LANES), BF16),
                   jax.ShapeDtypeStruct((N_LAT_B, DEPTH, PAST, 512), BF16)],
        compiler_params=_cparams(("arbitrary", "arbitrary")),
        name="cache_kv",
    )(cache_ckv, cache_kr_placed, w_ukv, mla_kn_g, b64)


def _attn_kernel(*refs, mla, has_cache):
    if has_cache:
        q_ref, ks_ref, vs_ref, kc_ref, vc_ref, o_ref = refs
    else:
        q_ref, ks_ref, vs_ref, o_ref = refs
    qk_w = LANES if mla else HD

    def v_with_ones(v_ref, blk, half):
        v = v_ref[:, LANES * blk:LANES * blk + LANES].astype(BF16)
        lane = lax.broadcasted_iota(I32, v.shape, 1)
        return jnp.where(lane >= HD if half == 1 else lane < HD, v, jnp.ones_like(v))

    halves = []
    for h in range(NH):
        kvh = h if mla else h // GQA_GROUP
        blk, half = kvh // 2, kvh % 2
        sl = slice(qk_w * kvh, qk_w * kvh + qk_w)
        qh = q_ref[:, qk_w * h:qk_w * h + qk_w]
        s = _dot_nt(qh, ks_ref[:, sl].astype(BF16))
        m = jnp.max(s, axis=-1, keepdims=True)
        if has_cache:
            sc = _dot_nt(qh, kc_ref[:, sl].astype(BF16))
            m = jnp.maximum(m, jnp.max(sc, axis=-1, keepdims=True))
        o = _dot(jnp.exp((s - m).astype(BF16)), v_with_ones(vs_ref, blk, half))
        if has_cache:
            o = o + _dot(jnp.exp((sc - m).astype(BF16)), v_with_ones(vc_ref, blk, half))
        den = o[:, 0:1] if half == 1 else o[:, HD:HD + 1]
        on = o / den
        if half != h % 2:
            on = pltpu.roll(on, HD, 1)
        halves.append(on)
    lane = lax.broadcasted_iota(I32, (halves[0].shape[0], LANES), 1)
    o_ref[...] = jnp.concatenate(
        [jnp.where(lane < HD, halves[2 * b], halves[2 * b + 1]).astype(BF16) for b in range(NH // 2)], axis=1)


def _attn_call(l, latent, mla, q, ks, vs, kc, vc):
    if latent:
        nb, nq, n_self, q0, k0 = N_LAT_B, S_LAT // TM, S_LAT, CTX_TILES, T_CTX // S_LAT
    else:
        nb, nq, n_self, q0, k0 = N_CTX_B, 1, S_CTX, 0, 0

    def qspec(w):
        return pl.BlockSpec((TM, w), lambda b, i: (q0 + b * nq + i, 0))

    def kspec(w):
        return pl.BlockSpec((n_self, w), lambda b, i: (k0 + b, 0))

    def cspec(w):
        return pl.BlockSpec((None, None, PAST, w), lambda b, i: (b, l, 0, 0))

    args = [q, ks, vs]
    specs = [qspec(q.shape[1]), kspec(ks.shape[1]), kspec(vs.shape[1])]
    if latent:
        args += [kc, vc]
        specs += [cspec(ks.shape[1]), cspec(vs.shape[1])]
    return pl.pallas_call(
        functools.partial(_attn_kernel, mla=mla, has_cache=latent),
        grid=(nb, nq),
        in_specs=specs,
        out_specs=pl.BlockSpec((TM, 512), lambda b, i: (b * nq + i, 0)),
        out_shape=jax.ShapeDtypeStruct((nb * nq * TM, 512), BF16),
        compiler_params=_cparams(("arbitrary", "arbitrary")),
        name=("mla" if mla else "gqa") + ("_lat" if latent else "_ctx"),
    )(*args)


def _conv_kernel(up_ref, uc_ref, un_ref, dw_ref, dwb_ref, lng_ref, lnb_ref, o_ref, win_ref, rot_ref):
    i = pl.program_id(0)
    in_seq = (i - CTX_TILES) % LAT_TILES_PER_B
    has_prev = jnp.logical_and(i >= CTX_TILES, in_seq != 0)
    has_next = jnp.logical_and(i >= CTX_TILES, in_seq != LAT_TILES_PER_B - 1)
    win_ref[0:16, :] = jnp.where(has_prev, up_ref[TM - 16:TM, :], 0.0)
    win_ref[16:16 + TM, :] = uc_ref[...]
    win_ref[16 + TM:32 + TM, :] = jnp.where(has_next, un_ref[0:16, :], 0.0)
    rot_rows = rot_ref.shape[1]
    for sh in range(1, SUB):
        rot_ref[sh - 1] = win_ref[sh:sh + rot_rows, :]
    acc = jnp.zeros((TM, CONV_CH), F32) + dwb_ref[...]
    for j in range(CONV_K):
        al, sh = (j + 1) // SUB * SUB, (j + 1) % SUB
        tap = win_ref[al:al + TM, :] if sh == 0 else rot_ref[sh - 1, al:al + TM, :]
        acc = acc + tap * dw_ref[j:j + 1, :]
    mu = jnp.mean(acc, axis=-1, keepdims=True)
    cen = acc - mu
    var = jnp.mean(cen * cen, axis=-1, keepdims=True)
    y = cen * lax.rsqrt(var + EPS) * lng_ref[...] + lnb_ref[...]
    o_ref[...] = (y * jax.nn.sigmoid(y)).astype(BF16)


def _conv_call(l, u, lw):
    def layer(a):
        return pl.BlockSpec((None,) + a.shape[1:], lambda i: (l,) + (0,) * (a.ndim - 1))

    ws = [lw["conv_dw"], lw["conv_dw_b"], lw["conv_ln_g"], lw["conv_ln_b"]]
    return pl.pallas_call(
        _conv_kernel,
        grid=(N_TILES,),
        in_specs=[
            pl.BlockSpec((TM, CONV_CH), lambda i: (jnp.maximum(i - 1, 0), 0)),
            pl.BlockSpec((TM, CONV_CH), lambda i: (i, 0)),
            pl.BlockSpec((TM, CONV_CH), lambda i: (jnp.minimum(i + 1, N_TILES - 1), 0)),
        ] + [layer(a) for a in ws],
        out_specs=pl.BlockSpec((TM, CONV_CH), lambda i: (i, 0)),
        out_shape=jax.ShapeDtypeStruct((T, CONV_CH), BF16),
        scratch_shapes=[pltpu.VMEM((TM + 32, CONV_CH), F32),
                        pltpu.VMEM((SUB - 1, TM + 24, CONV_CH), F32)],
        compiler_params=_cparams(("arbitrary",)),
        name="conv",
    )(u, u, u, *ws)


def _merge_kernel(x_ref, mod_ref, g1_ref, wg_ref, agc_ref, agl_ref, amc_ref, aml_ref, cv_ref,
                  wog_ref, wom_ref, wpw_ref, wout_ref, g2_ref, wr_hi_ref, wr_lo_ref, br_ref,
                  x1_ref, h2_ref, gt_ref, cnt_ref):
    i = pl.program_id(0)
    is_ctx = i < CTX_TILES
    x = x_ref[...]
    mod = mod_ref[...]
    h = _norm_mod(x, g1_ref[...], mod[:, D:2 * D], mod[:, 0:D]).astype(BF16)
    gates = jax.nn.sigmoid(_dot(h, wg_ref[...]))
    ag = jnp.where(is_ctx, agc_ref[...], agl_ref[...])
    am = jnp.where(is_ctx, amc_ref[...], aml_ref[...])
    merged = (gates[:, 0:D] * _dot(ag, wog_ref[...])
              + gates[:, D:2 * D] * _dot(am, wom_ref[...])
              + gates[:, 2 * D:3 * D] * _dot(cv_ref[...], wpw_ref[...]))
    x1 = x + mod[:, 2 * D:3 * D] * _dot(merged.astype(BF16), wout_ref[...])
    x1_ref[...] = x1
    h2 = _norm_mod(x1, g2_ref[...], mod[:, 4 * D:5 * D], mod[:, 3 * D:4 * D])
    h2_hi = h2.astype(BF16)
    h2_ref[...] = h2_hi
    h2_lo = (h2 - h2_hi.astype(F32)).astype(BF16)
    wr_hi = wr_hi_ref[...]
    logit = _dot_nt(wr_hi, h2_hi) + _dot_nt(wr_hi, h2_lo) + _dot_nt(wr_lo_ref[...], h2_hi) + br_ref[...]
    row = lax.broadcasted_iota(I32, logit.shape, 0).astype(F32)
    sels, vals = [], []
    for _ in range(TOP_K):
        m = jnp.max(logit, axis=0, keepdims=True)
        idx = jnp.min(jnp.where(logit == m, row, float(N_EXP)), axis=0, keepdims=True)
        sel = row == idx
        sels.append(sel)
        vals.append(m)
        logit = jnp.where(sel, -jnp.inf, logit)
    es = [jnp.exp(v - vals[0]) for v in vals]
    den = es[0] + es[1] + es[2] + es[3]
    gate = jnp.zeros(logit.shape, F32)
    for sel, e in zip(sels, es):
        gate = jnp.where(sel, e / den, gate)
    gt_ref[...] = gate
    n = jnp.sum((gate > 0.0).astype(F32), axis=1, keepdims=True)
    cnt_ref[...] = jnp.broadcast_to(n, (N_EXP, LANES)).astype(I32)


def _merge_call(l, x, mods, lw, ag_c, ag_l, am_c, am_l, cv):
    def layer(a):
        return pl.BlockSpec((None,) + a.shape[1:], lambda i: (l,) + (0,) * (a.ndim - 1))

    def rows(w):
        return pl.BlockSpec((TM, w), lambda i: (i, 0))

    ctx_rows = pl.BlockSpec((TM, 512), lambda i: (jnp.minimum(i, CTX_TILES - 1), 0))
    lat_rows = pl.BlockSpec((TM, 512), lambda i: (jnp.maximum(i - CTX_TILES, 0), 0))
    ws1 = [lw["norm1_g"], lw["wg"]]
    ws2 = [lw["w_o_gqa"], lw["w_o_mla"], lw["w_pw2"], lw["w_out"], lw["norm2_g"],
           lw["wr_hi"], lw["wr_lo"], lw["b_router"]]
    return pl.pallas_call(
        _merge_kernel,
        grid=(N_TILES,),
        in_specs=[rows(D), pl.BlockSpec((None, 1, 6 * D), lambda i: (l * 8 + _grp(i), 0, 0))]
        + [layer(a) for a in ws1] + [ctx_rows, lat_rows, ctx_rows, lat_rows, rows(512)]
        + [layer(a) for a in ws2],
        out_specs=[rows(D), rows(D), pl.BlockSpec((N_EXP, TM), lambda i: (0, i)),
                   pl.BlockSpec((None, N_EXP, LANES), lambda i: (i, 0, 0))],
        out_shape=[jax.ShapeDtypeStruct((T, D), F32), jax.ShapeDtypeStruct((T, D), BF16),
                   jax.ShapeDtypeStruct((N_EXP, T), F32),
                   jax.ShapeDtypeStruct((N_TILES, N_EXP, LANES), I32)],
        compiler_params=_cparams(("arbitrary",)),
        name="merge",
    )(x, mods, *ws1, ag_c, ag_l, am_c, am_l, cv, *ws2)


def _moe_tables(cnt):
    n8 = (cnt + SUB - 1) // SUB * SUB
    loc = jnp.cumsum(n8, axis=1) - n8
    tot = jnp.sum(n8, axis=0)
    ntile = (tot + PIECE - SUB + ET - 1) // ET
    seg = ntile * ET + SEG_SLACK
    off = jnp.cumsum(seg) - seg
    start = off[None, :] + jnp.cumsum(n8, axis=0) - n8
    npc = (n8 + PIECE - 1) // PIECE
    pc = jnp.cumsum(npc, axis=1)
    p = jnp.arange(N_PIECE_MAX, dtype=I32)[None, :, None]
    owner = jnp.logical_and((pc - npc)[:, None, :] <= p, p < pc[:, None, :])

    def of_owner(a):
        return jnp.sum(jnp.where(owner, a[:, None, :], 0), axis=2)

    k = p[:, :, 0] - of_owner(pc - npc)
    live = jnp.any(owner, axis=2)
    src = of_owner(loc) + PIECE * k
    dst = of_owner(start) + PIECE * k
    loc_c = (pc - npc) * PIECE
    src_c = of_owner(loc_c) + PIECE * k

    tcum = jnp.cumsum(ntile)
    g = jnp.arange(N_ET_MAX, dtype=I32)
    tile_owner = jnp.logical_and((tcum - ntile)[None, :] <= g[:, None], g[:, None] < tcum[None, :])
    trow = jnp.sum(jnp.where(tile_owner, off // ET + g[:, None] - (tcum - ntile), 0), axis=1)
    gstart = jnp.concatenate([jnp.zeros((1,), I32), tcum.astype(I32)])

    def col_table(a):
        return jnp.broadcast_to(a.astype(F32)[:, :, None], (N_TILES, N_EXP, LANES))

    def row_table(a):
        a = jnp.concatenate([a.astype(F32), jnp.zeros((N_TILES, LANES - N_EXP), F32)], axis=1)
        return jnp.broadcast_to(a[:, None, :], (N_TILES, SUB, LANES))

    return dict(src=jnp.where(live, src, 0).reshape(-1).astype(I32),
                src_c=jnp.where(live, src_c, 0).reshape(-1).astype(I32),
                dst=jnp.where(live, dst, 0).reshape(-1).astype(I32),
                n_piece=pc[:, -1].astype(I32), flush=(off + tot).astype(I32),
                first=(off // ET).astype(I32), ntile=ntile.astype(I32),
                trow=trow.astype(I32), gstart=gstart,
                lo_row=row_table(loc), hi_row=row_table(loc + n8),
                lo_col=col_table(loc_c), hi_col=col_table(loc_c + n8))


def _rank_in_tile(gate_t):
    sel = gate_t > 0.0
    r = lax.broadcasted_iota(I32, (TM, TM), 0)
    c = lax.broadcasted_iota(I32, (TM, TM), 1)
    upper = (r < c).astype(BF16)
    rank = _dot(sel.astype(BF16), upper)
    return jnp.where(sel, rank, -1e4)


def _piece_loop(n, body):
    def step(p, carry):
        body(p)
        return carry

    lax.fori_loop(0, n, step, 0)


def _dispatch_kernel(src_ref, dst_ref, npc_ref, flush_ref, gt_ref, lo_ref, hi_ref, h2_ref, xs_ref,
                     buf_ref, zero_ref, sem_ref, zsem_ref):
    i = pl.program_id(0)
    slot = i & 1

    def piece_copy(tile, p, s):
        src = src_ref[tile * N_PIECE_MAX + p]
        dst = dst_ref[tile * N_PIECE_MAX + p]
        return pltpu.make_async_copy(buf_ref.at[s, pl.ds(pl.multiple_of(src, SUB), PIECE)],
                                     xs_ref.at[pl.ds(pl.multiple_of(dst, SUB), PIECE)], sem_ref.at[s])

    @pl.when(i == 0)
    def _():
        zero_ref[...] = jnp.zeros(zero_ref.shape, F32)
        copies = [pltpu.make_async_copy(
            zero_ref, xs_ref.at[pl.ds(pl.multiple_of(flush_ref[e], SUB), FLUSH)], zsem_ref.at[0])
            for e in range(N_EXP)]
        for cp in copies:
            cp.start()
        for cp in copies:
            cp.wait()

    gate_t = gt_ref[...]
    rk = _rank_in_tile(gate_t).astype(BF16)
    g_hi, g_mid, g_lo = _split3(gate_t)
    h2 = h2_ref[...]
    lo = lo_ref[0:1, 0:N_EXP]
    hi = hi_ref[0:1, 0:N_EXP]
    for c in range(SORT_CHUNKS):
        r = (lax.broadcasted_iota(I32, (TM, N_EXP), 0) + c * TM).astype(F32)
        own = jnp.logical_and(r >= lo, r < hi)
        base = jnp.sum(jnp.where(own, lo, 0.0), axis=1, keepdims=True)
        owned = jnp.sum(jnp.where(own, 1.0, 0.0), axis=1, keepdims=True) > 0.0
        in_run = jnp.where(owned, r[:, 0:1] - base, -1.0)
        expand = own.astype(BF16)
        hit = _dot(expand, rk) == in_run
        gexp = _dot(expand, g_hi) + _dot(expand, g_mid) + _dot(expand, g_lo)
        w = jnp.sum(jnp.where(hit, gexp, 0.0), axis=1, keepdims=True)
        buf_ref[slot, c * TM:(c + 1) * TM, 0:D] = _dot(hit.astype(BF16), h2)
        buf_ref[slot, c * TM:(c + 1) * TM, D:XS_W] = jnp.broadcast_to(w, (TM, LANES))

    @pl.when(i > 0)
    def _():
        _piece_loop(npc_ref[i - 1], lambda p: piece_copy(i - 1, p, 1 - slot).wait())

    _piece_loop(npc_ref[i], lambda p: piece_copy(i, p, slot).start())

    @pl.when(i == pl.num_programs(0) - 1)
    def _():
        _piece_loop(npc_ref[i], lambda p: piece_copy(i, p, slot).wait())


def _dispatch_call(tb, gate_t, h2):
    tab = pl.BlockSpec((None, SUB, LANES), lambda i, *_: (i, 0, 0))
    return pl.pallas_call(
        _dispatch_kernel,
        grid_spec=pltpu.PrefetchScalarGridSpec(
            num_scalar_prefetch=4,
            grid=(N_TILES,),
            in_specs=[pl.BlockSpec((N_EXP, TM), lambda i, *_: (0, i)), tab, tab,
                      pl.BlockSpec((TM, D), lambda i, *_: (i, 0))],
            out_specs=pl.BlockSpec(memory_space=pl.ANY),
            scratch_shapes=[pltpu.VMEM((2, SORT_ROWS, XS_W), F32), pltpu.VMEM((FLUSH, XS_W), F32),
                            pltpu.SemaphoreType.DMA((2,)), pltpu.SemaphoreType.DMA((1,))]),
        out_shape=jax.ShapeDtypeStruct((N_SLOTS, XS_W), F32),
        compiler_params=_cparams(("arbitrary",)),
        name="moe_dispatch",
    )(tb["src"], tb["dst"], tb["n_piece"], tb["flush"], gate_t, tb["lo_row"], tb["hi_row"], h2)


def _expert_kernel(trow_ref, gstart_ref, bgu_ref, bdn_ref, xs_ref, wgu_ref, wdn_ref, ys_ref,
                   wgu_f32_ref, wdn_f32_ref, wgu_bf_ref, wdn_bf_ref, x_buf_ref, y_buf_ref,
                   wsem_ref, xsem_ref, ysem_ref, *, layer):
    e = pl.program_id(0)
    slot = e & 1

    def weight_copies(expert, s):
        return (pltpu.make_async_copy(wgu_ref.at[layer, expert], wgu_f32_ref.at[s], wsem_ref.at[0, s]),
                pltpu.make_async_copy(wdn_ref.at[layer, expert], wdn_f32_ref.at[s], wsem_ref.at[1, s]))

    n_live = gstart_ref[N_EXP]

    def x_copy(g, s):
        row = pl.multiple_of(trow_ref[g] * ET, ET)
        return pltpu.make_async_copy(xs_ref.at[pl.ds(row, ET)], x_buf_ref.at[s], xsem_ref.at[s])

    def y_copy(g, s):
        row = pl.multiple_of(trow_ref[g] * ET, ET)
        return pltpu.make_async_copy(y_buf_ref.at[s], ys_ref.at[pl.ds(row, ET)], ysem_ref.at[s])

    @pl.when(e == 0)
    def _():
        for cp in weight_copies(0, 0):
            cp.start(priority=1)

        @pl.when(n_live > 0)
        def _():
            x_copy(0, 0).start()

    @pl.when(e + 1 < pl.num_programs(0))
    def _():
        for cp in weight_copies(e + 1, 1 - slot):
            cp.start(priority=1)

    for cp in weight_copies(e, slot):
        cp.wait()
    for r in range(0, D, 128):
        wgu_bf_ref[r:r + 128, :] = wgu_f32_ref[slot, r:r + 128, :].astype(BF16)
    for r in range(0, FF, 128):
        wdn_bf_ref[r:r + 128, :] = wdn_f32_ref[slot, r:r + 128, :].astype(BF16)
    bgu = bgu_ref[...]
    bdn = bdn_ref[...]

    def row_tile(g, carry):
        s = g & 1

        @pl.when(g + 1 < n_live)
        def _():
            x_copy(g + 1, 1 - s).start()

        x_copy(g, s).wait()

        @pl.when(g >= 2)
        def _():
            y_copy(g - 2, s).wait()

        x = x_buf_ref[s, :, 0:D].astype(BF16)
        w = x_buf_ref[s, :, D:D + 1]
        gu = _dot(x, wgu_bf_ref[...]) + bgu
        gl = jnp.minimum(gu[:, 0:FF], SWIGLU_LIMIT)
        up = jnp.clip(gu[:, FF:2 * FF], -SWIGLU_LIMIT, SWIGLU_LIMIT)
        act = (up + 1.0) * (gl * jax.nn.sigmoid(SWIGLU_ALPHA * gl))
        y_buf_ref[s] = w * (_dot(act.astype(BF16), wdn_bf_ref[...]) + bdn)
        y_copy(g, s).start()
        return carry

    lax.fori_loop(gstart_ref[e], gstart_ref[e + 1], row_tile, 0)

    @pl.when(e == pl.num_programs(0) - 1)
    def _():
        @pl.when(n_live >= 2)
        def _():
            y_copy(n_live - 2, n_live & 1).wait()

        @pl.when(n_live >= 1)
        def _():
            y_copy(n_live - 1, (n_live - 1) & 1).wait()


def _expert_call(l, tb, xs, w_gu, b_gu, w_dn, b_dn):
    return pl.pallas_call(
        functools.partial(_expert_kernel, layer=l),
        grid_spec=pltpu.PrefetchScalarGridSpec(
            num_scalar_prefetch=2,
            grid=(N_EXP,),
            in_specs=[
                pl.BlockSpec((None, None, 1, 2 * FF), lambda e, *_: (l, e, 0, 0)),
                pl.BlockSpec((None, None, 1, D), lambda e, *_: (l, e, 0, 0)),
                pl.BlockSpec(memory_space=pl.ANY),
                pl.BlockSpec(memory_space=pl.ANY),
                pl.BlockSpec(memory_space=pl.ANY),
            ],
            out_specs=pl.BlockSpec(memory_space=pl.ANY),
            scratch_shapes=[pltpu.VMEM((2, D, 2 * FF), F32), pltpu.VMEM((2, FF, D), F32),
                            pltpu.VMEM((D, 2 * FF), BF16), pltpu.VMEM((FF, D), BF16),
                            pltpu.VMEM((2, ET, XS_W), F32), pltpu.VMEM((2, ET, D), F32),
                            pltpu.SemaphoreType.DMA((2, 2)), pltpu.SemaphoreType.DMA((2,)),
                            pltpu.SemaphoreType.DMA((2,))]),
        out_shape=jax.ShapeDtypeStruct((N_SLOTS, D), F32),
        compiler_params=_cparams(("arbitrary",)),
        name="moe_experts",
    )(tb["trow"], tb["gstart"], b_gu.reshape(DEPTH, N_EXP, 1, 2 * FF), b_dn.reshape(DEPTH, N_EXP, 1, D),
      xs, w_gu, w_dn)


def _combine_kernel(src_ref, dst_ref, npc_ref, gt_ref, lo_ref, hi_ref, x1_ref, mod_ref, ys_ref, o_ref,
                    buf_ref, sem_ref):
    i = pl.program_id(0)
    slot = i & 1

    def piece_copy(tile, p, s):
        src = src_ref[tile * N_PIECE_MAX + p]
        dst = dst_ref[tile * N_PIECE_MAX + p]
        return pltpu.make_async_copy(ys_ref.at[pl.ds(pl.multiple_of(dst, SUB), PIECE)],
                                     buf_ref.at[s, pl.ds(pl.multiple_of(src, SUB), PIECE)], sem_ref.at[s])

    @pl.when(i == 0)
    def _():
        buf_ref[...] = jnp.zeros(buf_ref.shape, F32)
        _piece_loop(npc_ref[0], lambda p: piece_copy(0, p, 0).start())

    @pl.when(i + 1 < pl.num_programs(0))
    def _():
        _piece_loop(npc_ref[i + 1], lambda p: piece_copy(i + 1, p, 1 - slot).start())

    rank_t = _rank_in_tile(gt_ref[...])
    r = lax.broadcasted_iota(I32, (TM, TM), 0)
    c = lax.broadcasted_iota(I32, (TM, TM), 1)
    eye = (r == c).astype(BF16)
    rk = _dot_nt(eye, rank_t.astype(BF16)).astype(BF16)
    lo = lo_ref[:, 0:1]
    hi = hi_ref[:, 0:1]

    _piece_loop(npc_ref[i], lambda p: piece_copy(i, p, slot).wait())

    def chunk(cidx, acc):
        rr = (lax.broadcasted_iota(I32, (N_EXP, TM), 1) + cidx * TM).astype(F32)
        own = jnp.logical_and(rr >= lo, rr < hi)
        base = jnp.sum(jnp.where(own, lo, 0.0), axis=0, keepdims=True)
        owned = jnp.sum(jnp.where(own, 1.0, 0.0), axis=0, keepdims=True) > 0.0
        in_run = jnp.where(owned, rr[0:1, :] - base, -1.0)
        hit = _dot(rk, own.astype(BF16)) == in_run
        y = buf_ref[slot, cidx * TM:(cidx + 1) * TM, :].astype(BF16)
        return acc + _dot(hit.astype(BF16), y)

    acc = jnp.zeros((TM, D), F32)
    for cidx in range(UNSORT_ROWS // TM):
        acc = chunk(cidx, acc)
    o_ref[...] = x1_ref[...] + mod_ref[...][:, 5 * D:6 * D] * acc


def _combine_call(l, tb, gate_t, x1, mods, ys):
    tab = pl.BlockSpec((None, N_EXP, LANES), lambda i, *_: (i, 0, 0))
    return pl.pallas_call(
        _combine_kernel,
        grid_spec=pltpu.PrefetchScalarGridSpec(
            num_scalar_prefetch=3,
            grid=(N_TILES,),
            in_specs=[
                pl.BlockSpec((N_EXP, TM), lambda i, *_: (0, i)), tab, tab,
                pl.BlockSpec((TM, D), lambda i, *_: (i, 0)),
                pl.BlockSpec((None, 1, 6 * D), lambda i, *_: (l * 8 + _grp(i), 0, 0)),
                pl.BlockSpec(memory_space=pl.ANY),
            ],
            out_specs=pl.BlockSpec((TM, D), lambda i, *_: (i, 0)),
            scratch_shapes=[pltpu.VMEM((2, UNSORT_ROWS, D), F32), pltpu.SemaphoreType.DMA((2,))]),
        out_shape=jax.ShapeDtypeStruct((T, D), F32),
        compiler_params=_cparams(("arbitrary",)),
        name="moe_combine",
    )(tb["src_c"], tb["dst"], tb["n_piece"], gate_t, tb["lo_col"], tb["hi_col"], x1, mods, ys)


def _rope_tables():
    t = jnp.arange(S_LAT, dtype=I32)
    row = (t // GRID_W).astype(F32)
    col = (t % GRID_W).astype(F32)

    def table(h, reps):
        inv = ROPE_THETA ** (-jnp.arange(0, h, 2, dtype=F32) / h)
        ar = row[:, None] * inv[None, :]
        ac = col[:, None] * inv[None, :]
        cos = jnp.concatenate([jnp.cos(ar), jnp.cos(ar), jnp.cos(ac), jnp.cos(ac)], axis=-1)
        sin = jnp.concatenate([-jnp.sin(ar), jnp.sin(ar), -jnp.sin(ac), jnp.sin(ac)], axis=-1)
        cos = jnp.tile(cos, (1, reps))
        sin = jnp.tile(sin, (1, reps))
        ident_c = jnp.ones((TM, LANES), F32)
        ident_s = jnp.zeros((TM, LANES), F32)
        return jnp.concatenate([ident_c, cos], axis=0), jnp.concatenate([ident_s, sin], axis=0)

    c64, s64 = table(HD // 2, 2)
    c32, s32 = table(MLA_ROPE // 2, 4)
    return c64, s64, c32, s32


def _group_matrix(n_rows, row_group, n_cols, col_group):
    r = np.arange(n_rows)[:, None] // row_group
    c = np.arange(n_cols)[None, :] // col_group
    return jnp.asarray((r == c).astype(np.float32), dtype=BF16)


def kernel(x_prompt, x_sample, cache_gqa_k, cache_gqa_v, cache_mla_ckv, cache_mla_krope, c, c_ctx,
           w_ada, b_ada, norm1_g, w_in, gqa_qn_g, gqa_kn_g, w_o_gqa, mla_cq_g, w_uq, mla_ckv_g, w_ukv,
           mla_qn_g, mla_kn_g, mla_krn_g, w_o_mla, conv_dw, conv_dw_b, conv_ln_g, conv_ln_b, w_pw2,
           w_out, norm2_g, w_router, b_router, w_gu, b_gu, w_dn, b_dn):
    L = DEPTH
    c64, s64, c32, s32 = _rope_tables()
    consts = dict(c64=c64, s64=s64, c32=c32, s32=s32,
                  b64=_group_matrix(512, HD, 512, HD), brn=_group_matrix(256, MLA_ROPE, 512, HD),
                  bnr=_group_matrix(512, HD, 256, MLA_ROPE),
                  brr=_group_matrix(256, MLA_ROPE, 256, MLA_ROPE))

    pad = jnp.zeros((L, D, LANES - MLA_ROPE), F32)
    wa = jnp.concatenate([w_in[:, :, 0:1440], pad, w_in[:, :, 1440:2464]], axis=-1).astype(BF16)
    wg = w_in[:, :, 2464:].astype(BF16)
    uq = w_uq.reshape(L, MLA_Q_RANK, NH, MLA_QK)
    uq = jnp.concatenate([uq[..., :HD].reshape(L, MLA_Q_RANK, NH * HD),
                          uq[..., HD:].reshape(L, MLA_Q_RANK, NH * MLA_ROPE)], axis=-1).astype(BF16)
    ukv = w_ukv.reshape(L, MLA_KV_RANK, NH, 2 * HD)
    ukv = jnp.concatenate([ukv[..., :HD].reshape(L, MLA_KV_RANK, NH * HD),
                           ukv[..., HD:].reshape(L, MLA_KV_RANK, NH * HD)], axis=-1).astype(BF16)

    def row3(a):
        return a.reshape(L, 1, a.shape[-1])

    wr_t = jnp.swapaxes(w_router, 1, 2)
    wr_hi = wr_t.astype(BF16)
    lw = dict(
        norm1_g=row3(norm1_g), wa=wa, wg=wg, w_uq=uq, w_ukv=ukv,
        gqa_qn_g=row3(jnp.tile(gqa_qn_g, (1, NH))), gqa_kn_g=row3(jnp.tile(gqa_kn_g, (1, GQA_KV))),
        mla_cq_g=row3(mla_cq_g), mla_ckv_g=row3(mla_ckv_g),
        mla_krn_g=row3(jnp.concatenate([mla_krn_g, jnp.zeros((L, LANES - MLA_ROPE), F32)], axis=-1)),
        mla_qn_n=row3(jnp.tile(mla_qn_g[:, :HD], (1, NH))),
        mla_qn_r=row3(jnp.tile(mla_qn_g[:, HD:], (1, NH))),
        mla_kn_g=row3(jnp.tile(mla_kn_g, (1, NH))),
        conv_dw=conv_dw, conv_dw_b=row3(conv_dw_b), conv_ln_g=row3(conv_ln_g), conv_ln_b=row3(conv_ln_b),
        w_o_gqa=w_o_gqa.astype(BF16), w_o_mla=w_o_mla.astype(BF16), w_pw2=w_pw2.astype(BF16),
        w_out=w_out.astype(BF16), norm2_g=row3(norm2_g),
        wr_hi=wr_hi, wr_lo=(wr_t - wr_hi.astype(F32)).astype(BF16),
        b_router=b_router.reshape(L, N_EXP, 1),
    )

    cvec = jnp.concatenate([c_ctx[None, :], c, jnp.zeros((8 - 1 - N_LAT_B, D), F32)], axis=0)
    mods = _ada_call(cvec, w_ada, b_ada).reshape(L * 8, 1, 6 * D)

    ck = cache_gqa_k.reshape(N_LAT_B, L, PAST, GQA_KV * HD)
    cv_ = cache_gqa_v.reshape(N_LAT_B, L, PAST, GQA_KV * HD)
    kr_placed = jnp.pad(cache_mla_krope, ((0, 0), (0, 0), (0, 0), (HD, LANES - MLA_QK)))
    km_c, vm_c = _cache_kv_call(cache_mla_ckv, kr_placed, ukv, lw["mla_kn_g"], consts["b64"])

    x = jnp.concatenate([x_prompt.reshape(T_CTX, D), x_sample.reshape(T_LAT, D)], axis=0)
    new_k, new_v, new_ckv, new_kr = [], [], [], []
    for l in range(L):
        qg, kg, vg, qm, ckv, kr, km, vm, u = _inproj_call(l, x, mods, lw, consts)
        new_k.append(kg[:T_CTX].reshape(N_CTX_B, S_CTX, GQA_KV, HD))
        new_v.append(vg[:T_CTX].reshape(N_CTX_B, S_CTX, GQA_KV, HD))
        new_ckv.append(ckv[:T_CTX].reshape(N_CTX_B, S_CTX, MLA_KV_RANK))
        new_kr.append(kr[:T_CTX, :MLA_ROPE].reshape(N_CTX_B, S_CTX, MLA_ROPE))
        ag_c = _attn_call(l, False, False, qg, kg, vg, None, None)
        ag_l = _attn_call(l, True, False, qg, kg, vg, ck, cv_)
        am_c = _attn_call(l, False, True, qm, km, vm, None, None)
        am_l = _attn_call(l, True, True, qm, km, vm, km_c, vm_c)
        cv = _conv_call(l, u, lw)
        x1, h2, gate_t, cnt = _merge_call(l, x, mods, lw, ag_c, ag_l, am_c, am_l, cv)
        tb = _moe_tables(cnt[:, :, 0])
        xs = _dispatch_call(tb, gate_t, h2)
        ys = _expert_call(l, tb, xs, w_gu, b_gu, w_dn, b_dn)
        x = _combine_call(l, tb, gate_t, x1, mods, ys)

    return (x[:T_CTX].reshape(N_CTX_B, S_CTX, D), x[T_CTX:].reshape(N_LAT_B, S_LAT, D),
            jnp.stack(new_k, axis=1), jnp.stack(new_v, axis=1),
            jnp.stack(new_ckv, axis=1), jnp.stack(new_kr, axis=1))
```

```python
import functools

import numpy as np
import jax
import jax.numpy as jnp
from jax import lax
from jax.experimental import pallas as pl
from jax.experimental.pallas import tpu as pltpu

F32 = jnp.float32
BF16 = jnp.bfloat16
I32 = jnp.int32

D = 1024
DEPTH = 4
N_CTX_B, S_CTX = 16, 256
N_LAT_B, S_LAT = 2, 2048
PAST = 256
GRID_W = 64
ROPE_THETA = 10000.0
EPS = 1e-6
HD = 64
NH = 8
GQA_KV = 2
GQA_GROUP = NH // GQA_KV
MLA_Q_RANK = 384
MLA_KV_RANK = 256
MLA_ROPE = 32
MLA_QK = HD + MLA_ROPE
CONV_CH = 512
CONV_K = 31
CONV_PAD = CONV_K // 2
N_EXP = 32
TOP_K = 4
FF = 1024
SWIGLU_LIMIT = 7.0
SWIGLU_ALPHA = 1.702

T_CTX = N_CTX_B * S_CTX
T_LAT = N_LAT_B * S_LAT
T = T_CTX + T_LAT

LANES = 128
TM = 256
N_TILES = T // TM
CTX_TILES = T_CTX // TM
LAT_TILES_PER_B = S_LAT // TM
CTX_SEQ_PER_STEP = 2
TILES_PER_STEP = 2
WA = 2560
GATES_W = 3 * D
SUB = 8
PIECE = 32
ET = 256
XS_W = D + LANES
SORT_ROWS = 1280
SORT_CHUNKS = SORT_ROWS // TM
assert TM * TOP_K + (SUB - 1) * N_EXP + PIECE - SUB <= SORT_ROWS
UNSORT_ROWS = 2048
assert TM * TOP_K + (PIECE - 1) * N_EXP <= UNSORT_ROWS
N_PIECE_MAX = (TM * TOP_K + (SUB - 1) * N_EXP) // PIECE + N_EXP
FLUSH = ET + PIECE
SEG_SLACK = 2 * ET
N_ET_MAX = (T * TOP_K + (SUB - 1) * N_TILES * N_EXP + N_EXP * (PIECE - SUB)) // ET + N_EXP
N_SLOTS = N_ET_MAX * ET + N_EXP * SEG_SLACK
VMEM_LIMIT = 56 * 1024 * 1024


def _dot(a, b):
    return jnp.dot(a, b, preferred_element_type=F32)


def _dot_nt(a, b):
    return lax.dot_general(a, b, (((1,), (1,)), ((), ())), preferred_element_type=F32)


def _split3(x):
    hi = x.astype(BF16)
    r = x - hi.astype(F32)
    mid = r.astype(BF16)
    lo = (r - mid.astype(F32)).astype(BF16)
    return hi, mid, lo


def _cparams(sem, vmem=VMEM_LIMIT):
    return pltpu.CompilerParams(dimension_semantics=sem, vmem_limit_bytes=vmem)


def _grp(i):
    return jnp.where(i < CTX_TILES, 0, 1 + (i - CTX_TILES) // LAT_TILES_PER_B)


def _pos_blk(i):
    return jnp.where(i < CTX_TILES, 0, 1 + (i - CTX_TILES) % LAT_TILES_PER_B)


def _ada_kernel(c_ref, w_ref, b_ref, o_ref):
    c = c_ref[...]
    s = c * jax.nn.sigmoid(c)
    w = w_ref[...]
    s_hi, s_lo, _ = _split3(s)
    w_hi, w_lo, _ = _split3(w)
    o_ref[...] = _dot(s_hi, w_hi) + _dot(s_hi, w_lo) + _dot(s_lo, w_hi) + b_ref[...]


def _ada_call(cvec, w_ada, b_ada):
    tn = 1536
    return pl.pallas_call(
        _ada_kernel,
        grid=(DEPTH, 6 * D // tn),
        in_specs=[
            pl.BlockSpec((8, D), lambda l, j: (0, 0)),
            pl.BlockSpec((None, D, tn), lambda l, j: (l, 0, j)),
            pl.BlockSpec((None, 1, tn), lambda l, j: (l, 0, j)),
        ],
        out_specs=pl.BlockSpec((None, 8, tn), lambda l, j: (l, 0, j)),
        out_shape=jax.ShapeDtypeStruct((DEPTH, 8, 6 * D), F32),
        compiler_params=_cparams(("arbitrary", "arbitrary")),
        name="ada_mod",
    )(cvec, w_ada, b_ada.reshape(DEPTH, 1, 6 * D))


def _norm_mod(x, g, scale, shift):
    ms = jnp.mean(x * x, axis=-1, keepdims=True)
    return (x * lax.rsqrt(ms + EPS) * g) * (1.0 + scale) + shift


def _group_rms(x, ss, n, g):
    return x * lax.rsqrt(ss * (1.0 / n) + EPS) * g


def _partner(x, dist):
    n = x.shape[-1]
    lane = lax.broadcasted_iota(I32, x.shape, 1)
    r1 = pltpu.roll(x, dist, 1)
    r2 = pltpu.roll(x, n - dist, 1)
    r1_is_minus = pltpu.roll(lane, dist, 1) == ((lane + (n - dist)) & (n - 1))
    plus = jnp.where(r1_is_minus, r2, r1)
    minus = jnp.where(r1_is_minus, r1, r2)
    return jnp.where((lane & (2 * dist - 1)) < dist, plus, minus)


def _rope(x, cos, sin_signed, dist):
    return x * cos + _partner(x, dist) * sin_signed


def _tile_lanes(x, reps):
    return jnp.concatenate([x] * reps, axis=1)


def _lanes_up(x, s):
    lane = lax.broadcasted_iota(I32, x.shape, 1)
    r1 = pltpu.roll(x, s, 1)
    r2 = pltpu.roll(x, LANES - s, 1)
    r1_moves_up = pltpu.roll(lane, s, 1) == ((lane + (LANES - s)) & (LANES - 1))
    return jnp.where(r1_moves_up, r1, r2)


def _mla_head_blocks(nope, rope_blocks):
    lane = lax.broadcasted_iota(I32, (nope.shape[0], LANES), 1)
    blocks = []
    for h in range(NH):
        pair = nope[:, LANES * (h // 2):LANES * (h // 2) + LANES]
        if h % 2 == 1:
            pair = pltpu.roll(pair, HD, 1)
        blocks.append(jnp.where(lane < HD, pair, jnp.where(lane < MLA_QK, rope_blocks[h], 0.0)))
    return jnp.concatenate(blocks, axis=1).astype(BF16)


def _inproj_kernel(x_ref, mod_ref, g1_ref, wa_ref, wuq_ref, wukv_ref,
                   qng_ref, kng_ref, cqg_ref, ckvg_ref, krg_ref, mqn_ref, mqr_ref, mkn_ref,
                   c64_ref, s64_ref, c32_ref, s32_ref,
                   b64_ref, brn_ref, bnr_ref, brr_ref,
                   qg_ref, kg_ref, vg_ref, qm_ref, ckv_ref, kr_ref, km_ref, vm_ref, u_ref):
    mod = mod_ref[...]
    h = _norm_mod(x_ref[...], g1_ref[...], mod[:, D:2 * D], mod[:, 0:D])
    p = _dot(h.astype(BF16), wa_ref[...])
    b64 = b64_ref[...]
    c64 = c64_ref[...]
    s64 = s64_ref[...]
    c32 = c32_ref[...]
    s32 = s32_ref[...]

    q = p[:, 0:512]
    q = _group_rms(q, _dot((q * q).astype(BF16), b64), HD, qng_ref[...])
    q = _rope(q, _tile_lanes(c64, 4), _tile_lanes(s64, 4), 16)
    qg_ref[...] = (q * (HD ** -0.5)).astype(BF16)
    k = p[:, 512:640]
    k = _group_rms(k, _dot((k * k).astype(BF16), b64[0:128, 0:128]), HD, kng_ref[...])
    kg_ref[...] = _rope(k, c64, s64, 16)
    vg_ref[...] = p[:, 640:768]

    cq = p[:, 768:1152]
    cq = cq * lax.rsqrt(jnp.mean(cq * cq, axis=-1, keepdims=True) + EPS) * cqg_ref[...]
    qm = _dot(cq.astype(BF16), wuq_ref[...])
    qmn = qm[:, 0:512]
    qmr = qm[:, 512:768]
    sqn = (qmn * qmn).astype(BF16)
    sqr = (qmr * qmr).astype(BF16)
    ss_n = _dot(sqn, b64) + _dot(sqr, brn_ref[...])
    ss_r = _dot(sqn, bnr_ref[...]) + _dot(sqr, brr_ref[...])
    scale = MLA_QK ** -0.5
    qmn = _group_rms(qmn, ss_n, MLA_QK, mqn_ref[...]) * scale
    qmr = _group_rms(qmr, ss_r, MLA_QK, mqr_ref[...])
    qmr = _rope(qmr, _tile_lanes(c32, 2), _tile_lanes(s32, 2), 8) * scale
    q_rope = []
    for hd in range(NH):
        blk = qmr[:, LANES * (hd // 4):LANES * (hd // 4) + LANES]
        up = (HD - MLA_ROPE * (hd % 4)) % LANES
        q_rope.append(_lanes_up(blk, up) if up else blk)
    qm_ref[...] = _mla_head_blocks(qmn, q_rope)

    kr = p[:, 1408:1536]
    kr = kr * lax.rsqrt(jnp.sum(kr * kr, axis=-1, keepdims=True) * (1.0 / MLA_ROPE) + EPS) * krg_ref[...]
    kr = _rope(kr, c32, s32, 8)
    kr_ref[...] = kr

    ckv = p[:, 1152:1408]
    ckv = ckv * lax.rsqrt(jnp.mean(ckv * ckv, axis=-1, keepdims=True) + EPS) * ckvg_ref[...]
    ckv_ref[...] = ckv
    kv = _dot(ckv.astype(BF16), wukv_ref[...])
    kn = kv[:, 0:512]
    kn = _group_rms(kn, _dot((kn * kn).astype(BF16), b64), HD, mkn_ref[...])
    km_ref[...] = _mla_head_blocks(kn, [pltpu.roll(kr, HD, 1)] * NH)
    vm_ref[...] = kv[:, 512:1024].astype(BF16)

    u_ref[...] = p[:, 1536:2048] * jax.nn.sigmoid(p[:, 2048:2560])


def _inproj_call(l, x, mods, lw, consts):
    def full(a):
        return pl.BlockSpec(a.shape, lambda i: (0,) * a.ndim)

    def layer(a):
        return pl.BlockSpec((None,) + a.shape[1:], lambda i: (l,) + (0,) * (a.ndim - 1))

    def rows(w):
        return pl.BlockSpec((TM, w), lambda i: (i, 0))

    def tab():
        return pl.BlockSpec((TM, LANES), lambda i: (_pos_blk(i), 0))

    layer_ws = [lw["norm1_g"], lw["wa"], lw["w_uq"], lw["w_ukv"], lw["gqa_qn_g"], lw["gqa_kn_g"],
                lw["mla_cq_g"], lw["mla_ckv_g"], lw["mla_krn_g"], lw["mla_qn_n"], lw["mla_qn_r"],
                lw["mla_kn_g"]]
    tabs = [consts["c64"], consts["s64"], consts["c32"], consts["s32"]]
    bms = [consts["b64"], consts["brn"], consts["bnr"], consts["brr"]]
    out_w = [(512, BF16), (128, F32), (128, F32), (NH * LANES, BF16), (256, F32), (128, F32),
             (NH * LANES, BF16), (512, BF16), (512, F32)]
    return pl.pallas_call(
        _inproj_kernel,
        grid=(N_TILES,),
        in_specs=[rows(D), pl.BlockSpec((None, 1, 6 * D), lambda i: (l * 8 + _grp(i), 0, 0))]
        + [layer(a) for a in layer_ws] + [tab() for _ in tabs] + [full(a) for a in bms],
        out_specs=[rows(w) for w, _ in out_w],
        out_shape=[jax.ShapeDtypeStruct((T, w), dt) for w, dt in out_w],
        compiler_params=_cparams(("arbitrary",)),
        name="inproj",
    )(x, mods, *layer_ws, *tabs, *bms)


def _cache_kv_kernel(ckv_ref, kr_ref, wukv_ref, mkn_ref, b64_ref, km_ref, vm_ref):
    kv = _dot(ckv_ref[...].astype(BF16), wukv_ref[...])
    kn = kv[:, 0:512]
    kn = _group_rms(kn, _dot((kn * kn).astype(BF16), b64_ref[...]), HD, mkn_ref[...])
    km_ref[...] = _mla_head_blocks(kn, [kr_ref[...]] * NH)
    vm_ref[...] = kv[:, 512:1024].astype(BF16)


def _cache_kv_call(cache_ckv, cache_kr_placed, w_ukv, mla_kn_g, b64):
    def out(w):
        return pl.BlockSpec((None, None, PAST, w), lambda b, l: (b, l, 0, 0))

    return pl.pallas_call(
        _cache_kv_kernel,
        grid=(N_LAT_B, DEPTH),
        in_specs=[
            out(MLA_KV_RANK), out(LANES),
            pl.BlockSpec((None, MLA_KV_RANK, 1024), lambda b, l: (l, 0, 0)),
            pl.BlockSpec((None, 1, 512), lambda b, l: (l, 0, 0)),
            pl.BlockSpec((512, 512), lambda b, l: (0, 0)),
        ],
        out_specs=[out(NH * LANES), out(512)],
        out_shape=[jax.ShapeDtypeStruct((N_LAT_B, DEPTH, PAST, NH * LANES), BF16),
                   jax.ShapeDtypeStruct((N_LAT_B, DEPTH, PAST, 512), BF16)],
        compiler_params=_cparams(("arbitrary", "arbitrary")),
        name="cache_kv",
    )(cache_ckv, cache_kr_placed, w_ukv, mla_kn_g, b64)


def _attn_kernel(*refs, mla, has_cache, n_seq):
    if has_cache:
        q_ref, ks_ref, vs_ref, kc_ref, vc_ref, o_ref = refs
    else:
        q_ref, ks_ref, vs_ref, o_ref = refs
    qk_w = LANES if mla else HD
    n_k = ks_ref.shape[0] // n_seq

    def v_with_ones(v_ref, rows, blk, half):
        v = v_ref[rows, LANES * blk:LANES * blk + LANES].astype(BF16)
        lane = lax.broadcasted_iota(I32, v.shape, 1)
        return jnp.where(lane >= HD if half == 1 else lane < HD, v, jnp.ones_like(v))

    halves = [[] for _ in range(n_seq)]
    for j in range(n_seq):
        qrows = slice(j * TM, (j + 1) * TM)
        krows = slice(j * n_k, (j + 1) * n_k)
        for h in range(NH):
            kvh = h if mla else h // GQA_GROUP
            blk, half = kvh // 2, kvh % 2
            sl = slice(qk_w * kvh, qk_w * kvh + qk_w)
            qh = q_ref[qrows, qk_w * h:qk_w * h + qk_w]
            s = _dot_nt(qh, ks_ref[krows, sl].astype(BF16))
            m = jnp.max(s, axis=-1, keepdims=True)
            if has_cache:
                sc = _dot_nt(qh, kc_ref[:, sl].astype(BF16))
                m = jnp.maximum(m, jnp.max(sc, axis=-1, keepdims=True))
            o = _dot(jnp.exp((s - m).astype(BF16)), v_with_ones(vs_ref, krows, blk, half))
            if has_cache:
                o = o + _dot(jnp.exp((sc - m).astype(BF16)), v_with_ones(vc_ref, slice(None), blk, half))
            den = o[:, 0:1] if half == 1 else o[:, HD:HD + 1]
            on = o / den
            if half != h % 2:
                on = pltpu.roll(on, HD, 1)
            halves[j].append(on)
    lane = lax.broadcasted_iota(I32, (TM, LANES), 1)
    for j in range(n_seq):
        o_ref[j * TM:(j + 1) * TM, :] = jnp.concatenate(
            [jnp.where(lane < HD, halves[j][2 * b], halves[j][2 * b + 1]).astype(BF16)
             for b in range(NH // 2)], axis=1)


def _attn_call(l, latent, mla, q, ks, vs, kc, vc):
    if latent:
        nb, nq, n_seq, n_self, q0, k0 = N_LAT_B, S_LAT // TM, 1, S_LAT, CTX_TILES, T_CTX // S_LAT
    else:
        n_seq = CTX_SEQ_PER_STEP
        nb, nq, n_self, q0, k0 = N_CTX_B // n_seq, 1, n_seq * S_CTX, 0, 0
    tq = n_seq * TM

    def qspec(w):
        return pl.BlockSpec((tq, w), lambda b, i: (q0 + b * nq + i, 0))

    def kspec(w):
        return pl.BlockSpec((n_self, w), lambda b, i: (k0 + b, 0))

    def cspec(w):
        return pl.BlockSpec((None, None, PAST, w), lambda b, i: (b, l, 0, 0))

    args = [q, ks, vs]
    specs = [qspec(q.shape[1]), kspec(ks.shape[1]), kspec(vs.shape[1])]
    if latent:
        args += [kc, vc]
        specs += [cspec(ks.shape[1]), cspec(vs.shape[1])]
    return pl.pallas_call(
        functools.partial(_attn_kernel, mla=mla, has_cache=latent, n_seq=n_seq),
        grid=(nb, nq),
        in_specs=specs,
        out_specs=pl.BlockSpec((tq, 512), lambda b, i: (b * nq + i, 0)),
        out_shape=jax.ShapeDtypeStruct((nb * nq * tq, 512), BF16),
        compiler_params=_cparams(("arbitrary", "arbitrary")),
        name=("mla" if mla else "gqa") + ("_lat" if latent else "_ctx"),
    )(*args)


def _conv_tile(tile, prev16, cur, next16, dw_ref, dwb_ref, lng_ref, lnb_ref, win_ref, rot_ref):
    in_seq = (tile - CTX_TILES) % LAT_TILES_PER_B
    has_prev = jnp.logical_and(tile >= CTX_TILES, in_seq != 0)
    has_next = jnp.logical_and(tile >= CTX_TILES, in_seq != LAT_TILES_PER_B - 1)
    win_ref[0:16, :] = jnp.where(has_prev, prev16, 0.0)
    win_ref[16:16 + TM, :] = cur
    win_ref[16 + TM:32 + TM, :] = jnp.where(has_next, next16, 0.0)
    rot_rows = rot_ref.shape[1]
    for sh in range(1, SUB):
        rot_ref[sh - 1] = win_ref[sh:sh + rot_rows, :]
    acc = jnp.zeros((TM, CONV_CH), F32) + dwb_ref[...]
    for j in range(CONV_K):
        al, sh = (j + 1) // SUB * SUB, (j + 1) % SUB
        tap = win_ref[al:al + TM, :] if sh == 0 else rot_ref[sh - 1, al:al + TM, :]
        acc = acc + tap * dw_ref[j:j + 1, :]
    mu = jnp.mean(acc, axis=-1, keepdims=True)
    cen = acc - mu
    var = jnp.mean(cen * cen, axis=-1, keepdims=True)
    y = cen * lax.rsqrt(var + EPS) * lng_ref[...] + lnb_ref[...]
    return (y * jax.nn.sigmoid(y)).astype(BF16)


def _merge_kernel(x_ref, mod_ref, g1_ref, wg_ref, agc_ref, agl_ref, amc_ref, aml_ref,
                  up_ref, uc_ref, un_ref, dw_ref, dwb_ref, lng_ref, lnb_ref,
                  wog_ref, wom_ref, wpw_ref, wout_ref, g2_ref, wr_hi_ref, wr_lo_ref, br_ref,
                  x1_ref, h2_ref, gt_ref, cnt_ref, win_ref, rot_ref):
    step = pl.program_id(0)
    is_ctx = step < CTX_TILES // TILES_PER_STEP
    mod = mod_ref[...]
    last = TILES_PER_STEP - 1
    for j in range(TILES_PER_STEP):
        rows = slice(j * TM, (j + 1) * TM)
        prev16 = up_ref[TM - 16:TM, :] if j == 0 else uc_ref[j * TM - 16:j * TM, :]
        next16 = un_ref[0:16, :] if j == last else uc_ref[(j + 1) * TM:(j + 1) * TM + 16, :]
        cv = _conv_tile(step * TILES_PER_STEP + j, prev16, uc_ref[rows, :], next16,
                        dw_ref, dwb_ref, lng_ref, lnb_ref, win_ref.at[j], rot_ref.at[j])
        _merge_tile(rows, j, is_ctx, mod, cv, x_ref, g1_ref, wg_ref, agc_ref, agl_ref,
                    amc_ref, aml_ref, wog_ref, wom_ref, wpw_ref, wout_ref, g2_ref, wr_hi_ref,
                    wr_lo_ref, br_ref, x1_ref, h2_ref, gt_ref, cnt_ref)


def _merge_tile(rows, j, is_ctx, mod, cv, x_ref, g1_ref, wg_ref, agc_ref, agl_ref, amc_ref, aml_ref,
                wog_ref, wom_ref, wpw_ref, wout_ref, g2_ref, wr_hi_ref, wr_lo_ref, br_ref,
                x1_ref, h2_ref, gt_ref, cnt_ref):
    x = x_ref[rows, :]
    h = _norm_mod(x, g1_ref[...], mod[:, D:2 * D], mod[:, 0:D]).astype(BF16)
    gates = jax.nn.sigmoid(_dot(h, wg_ref[...]))
    ag = jnp.where(is_ctx, agc_ref[rows, :], agl_ref[rows, :])
    am = jnp.where(is_ctx, amc_ref[rows, :], aml_ref[rows, :])
    merged = (gates[:, 0:D] * _dot(ag, wog_ref[...])
              + gates[:, D:2 * D] * _dot(am, wom_ref[...])
              + gates[:, 2 * D:3 * D] * _dot(cv, wpw_ref[...]))
    x1 = x + mod[:, 2 * D:3 * D] * _dot(merged.astype(BF16), wout_ref[...])
    x1_ref[rows, :] = x1
    h2 = _norm_mod(x1, g2_ref[...], mod[:, 4 * D:5 * D], mod[:, 3 * D:4 * D])
    h2_hi = h2.astype(BF16)
    h2_ref[rows, :] = h2_hi
    h2_lo = (h2 - h2_hi.astype(F32)).astype(BF16)
    wr_hi = wr_hi_ref[...]
    logit = _dot_nt(wr_hi, h2_hi) + _dot_nt(wr_hi, h2_lo) + _dot_nt(wr_lo_ref[...], h2_hi) + br_ref[...]
    row = lax.broadcasted_iota(I32, logit.shape, 0).astype(F32)
    sels, vals = [], []
    for _ in range(TOP_K):
        m = jnp.max(logit, axis=0, keepdims=True)
        idx = jnp.min(jnp.where(logit == m, row, float(N_EXP)), axis=0, keepdims=True)
        sel = row == idx
        sels.append(sel)
        vals.append(m)
        logit = jnp.where(sel, -jnp.inf, logit)
    es = [jnp.exp(v - vals[0]) for v in vals]
    den = es[0] + es[1] + es[2] + es[3]
    gate = jnp.zeros(logit.shape, F32)
    for sel, e in zip(sels, es):
        gate = jnp.where(sel, e / den, gate)
    gt_ref[:, rows] = gate
    n = jnp.sum((gate > 0.0).astype(F32), axis=1, keepdims=True)
    cnt_ref[j] = jnp.broadcast_to(n, (N_EXP, LANES)).astype(I32)


def _merge_call(l, x, mods, lw, ag_c, ag_l, am_c, am_l, u):
    tps = TILES_PER_STEP
    tr = tps * TM

    def layer(a):
        return pl.BlockSpec((None,) + a.shape[1:], lambda i: (l,) + (0,) * (a.ndim - 1))

    def rows(w):
        return pl.BlockSpec((tr, w), lambda i: (i, 0))

    ctx_rows = pl.BlockSpec((tr, 512), lambda i: (jnp.minimum(i, CTX_TILES // tps - 1), 0))
    lat_rows = pl.BlockSpec((tr, 512), lambda i: (jnp.maximum(i - CTX_TILES // tps, 0), 0))
    u_prev = pl.BlockSpec((TM, CONV_CH), lambda i: (jnp.maximum(i * tps - 1, 0), 0))
    u_next = pl.BlockSpec((TM, CONV_CH), lambda i: (jnp.minimum((i + 1) * tps, N_TILES - 1), 0))
    ws1 = [lw["norm1_g"], lw["wg"]]
    wsc = [lw["conv_dw"], lw["conv_dw_b"], lw["conv_ln_g"], lw["conv_ln_b"]]
    ws2 = [lw["w_o_gqa"], lw["w_o_mla"], lw["w_pw2"], lw["w_out"], lw["norm2_g"],
           lw["wr_hi"], lw["wr_lo"], lw["b_router"]]
    return pl.pallas_call(
        _merge_kernel,
        grid=(N_TILES // tps,),
        in_specs=[rows(D), pl.BlockSpec((None, 1, 6 * D), lambda i: (l * 8 + _grp(i * tps), 0, 0))]
        + [layer(a) for a in ws1] + [ctx_rows, lat_rows, ctx_rows, lat_rows, u_prev, rows(CONV_CH), u_next]
        + [layer(a) for a in wsc] + [layer(a) for a in ws2],
        scratch_shapes=[pltpu.VMEM((tps, TM + 32, CONV_CH), F32),
                        pltpu.VMEM((tps, SUB - 1, TM + 24, CONV_CH), F32)],
        out_specs=[rows(D), rows(D), pl.BlockSpec((N_EXP, tr), lambda i: (0, i)),
                   pl.BlockSpec((tps, N_EXP, LANES), lambda i: (i, 0, 0))],
        out_shape=[jax.ShapeDtypeStruct((T, D), F32), jax.ShapeDtypeStruct((T, D), BF16),
                   jax.ShapeDtypeStruct((N_EXP, T), F32),
                   jax.ShapeDtypeStruct((N_TILES, N_EXP, LANES), I32)],
        compiler_params=_cparams(("arbitrary",)),
        name="merge",
    )(x, mods, *ws1, ag_c, ag_l, am_c, am_l, u, u, u, *wsc, *ws2)


def _moe_tables(cnt):
    n8 = (cnt + SUB - 1) // SUB * SUB
    loc = jnp.cumsum(n8, axis=1) - n8
    tot = jnp.sum(n8, axis=0)
    ntile = (tot + PIECE - SUB + ET - 1) // ET
    seg = ntile * ET + SEG_SLACK
    off = jnp.cumsum(seg) - seg
    start = off[None, :] + jnp.cumsum(n8, axis=0) - n8
    npc = (n8 + PIECE - 1) // PIECE
    pc = jnp.cumsum(npc, axis=1)
    p = jnp.arange(N_PIECE_MAX, dtype=I32)[None, :, None]
    owner = jnp.logical_and((pc - npc)[:, None, :] <= p, p < pc[:, None, :])

    def of_owner(a):
        return jnp.sum(jnp.where(owner, a[:, None, :], 0), axis=2)

    k = p[:, :, 0] - of_owner(pc - npc)
    live = jnp.any(owner, axis=2)
    src = of_owner(loc) + PIECE * k
    dst = of_owner(start) + PIECE * k
    loc_c = (pc - npc) * PIECE
    src_c = of_owner(loc_c) + PIECE * k

    tcum = jnp.cumsum(ntile)
    g = jnp.arange(N_ET_MAX, dtype=I32)
    tile_owner = jnp.logical_and((tcum - ntile)[None, :] <= g[:, None], g[:, None] < tcum[None, :])
    trow = jnp.sum(jnp.where(tile_owner, off // ET + g[:, None] - (tcum - ntile), 0), axis=1)
    gstart = jnp.concatenate([jnp.zeros((1,), I32), tcum.astype(I32)])

    def col_table(a):
        return jnp.broadcast_to(a.astype(F32)[:, :, None], (N_TILES, N_EXP, LANES))

    def row_table(a):
        a = jnp.concatenate([a.astype(F32), jnp.zeros((N_TILES, LANES - N_EXP), F32)], axis=1)
        return jnp.broadcast_to(a[:, None, :], (N_TILES, SUB, LANES))

    return dict(src=jnp.where(live, src, 0).reshape(-1).astype(I32),
                src_c=jnp.where(live, src_c, 0).reshape(-1).astype(I32),
                dst=jnp.where(live, dst, 0).reshape(-1).astype(I32),
                n_piece=pc[:, -1].astype(I32), flush=(off + tot).astype(I32),
                first=(off // ET).astype(I32), ntile=ntile.astype(I32),
                trow=trow.astype(I32), gstart=gstart,
                lo_row=row_table(loc), hi_row=row_table(loc + n8),
                lo_col=col_table(loc_c), hi_col=col_table(loc_c + n8))


def _rank_in_tile(gate_t):
    sel = gate_t > 0.0
    r = lax.broadcasted_iota(I32, (TM, TM), 0)
    c = lax.broadcasted_iota(I32, (TM, TM), 1)
    upper = (r < c).astype(BF16)
    rank = _dot(sel.astype(BF16), upper)
    return jnp.where(sel, rank, -1e4)


def _piece_loop(n, body):
    def step(p, carry):
        body(p)
        return carry

    lax.fori_loop(0, n, step, 0)


def _dispatch_kernel(src_ref, dst_ref, npc_ref, flush_ref, gt_ref, lo_ref, hi_ref, h2_ref, xs_ref,
                     buf_ref, zero_ref, sem_ref, zsem_ref):
    i = pl.program_id(0)
    slot = i & 1

    def piece_copy(tile, p, s):
        src = src_ref[tile * N_PIECE_MAX + p]
        dst = dst_ref[tile * N_PIECE_MAX + p]
        return pltpu.make_async_copy(buf_ref.at[s, pl.ds(pl.multiple_of(src, SUB), PIECE)],
                                     xs_ref.at[pl.ds(pl.multiple_of(dst, SUB), PIECE)], sem_ref.at[s])

    @pl.when(i == 0)
    def _():
        zero_ref[...] = jnp.zeros(zero_ref.shape, F32)
        copies = [pltpu.make_async_copy(
            zero_ref, xs_ref.at[pl.ds(pl.multiple_of(flush_ref[e], SUB), FLUSH)], zsem_ref.at[0])
            for e in range(N_EXP)]
        for cp in copies:
            cp.start()
        for cp in copies:
            cp.wait()

    gate_t = gt_ref[...]
    rk = _rank_in_tile(gate_t).astype(BF16)
    g_hi, g_mid, g_lo = _split3(gate_t)
    h2 = h2_ref[...]
    lo = lo_ref[0:1, 0:N_EXP]
    hi = hi_ref[0:1, 0:N_EXP]
    for c in range(SORT_CHUNKS):
        r = (lax.broadcasted_iota(I32, (TM, N_EXP), 0) + c * TM).astype(F32)
        own = jnp.logical_and(r >= lo, r < hi)
        base = jnp.sum(jnp.where(own, lo, 0.0), axis=1, keepdims=True)
        owned = jnp.sum(jnp.where(own, 1.0, 0.0), axis=1, keepdims=True) > 0.0
        in_run = jnp.where(owned, r[:, 0:1] - base, -1.0)
        expand = own.astype(BF16)
        hit = _dot(expand, rk) == in_run
        gexp = _dot(expand, g_hi) + _dot(expand, g_mid) + _dot(expand, g_lo)
        w = jnp.sum(jnp.where(hit, gexp, 0.0), axis=1, keepdims=True)
        buf_ref[slot, c * TM:(c + 1) * TM, 0:D] = _dot(hit.astype(BF16), h2)
        buf_ref[slot, c * TM:(c + 1) * TM, D:XS_W] = jnp.broadcast_to(w, (TM, LANES))

    @pl.when(i > 0)
    def _():
        _piece_loop(npc_ref[i - 1], lambda p: piece_copy(i - 1, p, 1 - slot).wait())

    _piece_loop(npc_ref[i], lambda p: piece_copy(i, p, slot).start())

    @pl.when(i == pl.num_programs(0) - 1)
    def _():
        _piece_loop(npc_ref[i], lambda p: piece_copy(i, p, slot).wait())


def _dispatch_call(tb, gate_t, h2):
    tab = pl.BlockSpec((None, SUB, LANES), lambda i, *_: (i, 0, 0))
    return pl.pallas_call(
        _dispatch_kernel,
        grid_spec=pltpu.PrefetchScalarGridSpec(
            num_scalar_prefetch=4,
            grid=(N_TILES,),
            in_specs=[pl.BlockSpec((N_EXP, TM), lambda i, *_: (0, i)), tab, tab,
                      pl.BlockSpec((TM, D), lambda i, *_: (i, 0))],
            out_specs=pl.BlockSpec(memory_space=pl.ANY),
            scratch_shapes=[pltpu.VMEM((2, SORT_ROWS, XS_W), F32), pltpu.VMEM((FLUSH, XS_W), F32),
                            pltpu.SemaphoreType.DMA((2,)), pltpu.SemaphoreType.DMA((1,))]),
        out_shape=jax.ShapeDtypeStruct((N_SLOTS, XS_W), F32),
        compiler_params=_cparams(("arbitrary",)),
        name="moe_dispatch",
    )(tb["src"], tb["dst"], tb["n_piece"], tb["flush"], gate_t, tb["lo_row"], tb["hi_row"], h2)


def _expert_kernel(trow_ref, gstart_ref, bgu_ref, bdn_ref, xs_ref, wgu_ref, wdn_ref, ys_ref,
                   wgu_f32_ref, wdn_f32_ref, wgu_bf_ref, wdn_bf_ref, x_buf_ref, y_buf_ref,
                   wsem_ref, xsem_ref, ysem_ref, *, layer):
    e = pl.program_id(0)
    slot = e & 1

    def weight_copies(expert, s):
        return (pltpu.make_async_copy(wgu_ref.at[layer, expert], wgu_f32_ref.at[s], wsem_ref.at[0, s]),
                pltpu.make_async_copy(wdn_ref.at[layer, expert], wdn_f32_ref.at[s], wsem_ref.at[1, s]))

    n_live = gstart_ref[N_EXP]

    def x_copy(g, s):
        row = pl.multiple_of(trow_ref[g] * ET, ET)
        return pltpu.make_async_copy(xs_ref.at[pl.ds(row, ET)], x_buf_ref.at[s], xsem_ref.at[s])

    def y_copy(g, s):
        row = pl.multiple_of(trow_ref[g] * ET, ET)
        return pltpu.make_async_copy(y_buf_ref.at[s], ys_ref.at[pl.ds(row, ET)], ysem_ref.at[s])

    @pl.when(e == 0)
    def _():
        for cp in weight_copies(0, 0):
            cp.start(priority=1)

        @pl.when(n_live > 0)
        def _():
            x_copy(0, 0).start()

    @pl.when(e + 1 < pl.num_programs(0))
    def _():
        for cp in weight_copies(e + 1, 1 - slot):
            cp.start(priority=1)

    for cp in weight_copies(e, slot):
        cp.wait()
    for r in range(0, D, 128):
        wgu_bf_ref[r:r + 128, :] = wgu_f32_ref[slot, r:r + 128, :].astype(BF16)
    for r in range(0, FF, 128):
        wdn_bf_ref[r:r + 128, :] = wdn_f32_ref[slot, r:r + 128, :].astype(BF16)
    bgu = bgu_ref[...]
    bdn = bdn_ref[...]

    def row_tile(g, carry):
        s = g & 1

        @pl.when(g + 1 < n_live)
        def _():
            x_copy(g + 1, 1 - s).start()

        x_copy(g, s).wait()

        @pl.when(g >= 2)
        def _():
            y_copy(g - 2, s).wait()

        x = x_buf_ref[s, :, 0:D].astype(BF16)
        w = x_buf_ref[s, :, D:D + 1]
        gu = _dot(x, wgu_bf_ref[...]) + bgu
        gl = jnp.minimum(gu[:, 0:FF], SWIGLU_LIMIT)
        up = jnp.clip(gu[:, FF:2 * FF], -SWIGLU_LIMIT, SWIGLU_LIMIT)
        act = (up + 1.0) * (gl * jax.nn.sigmoid(SWIGLU_ALPHA * gl))
        y_buf_ref[s] = w * (_dot(act.astype(BF16), wdn_bf_ref[...]) + bdn)
        y_copy(g, s).start()
        return carry

    lax.fori_loop(gstart_ref[e], gstart_ref[e + 1], row_tile, 0)

    @pl.when(e == pl.num_programs(0) - 1)
    def _():
        @pl.when(n_live >= 2)
        def _():
            y_copy(n_live - 2, n_live & 1).wait()

        @pl.when(n_live >= 1)
        def _():
            y_copy(n_live - 1, (n_live - 1) & 1).wait()


def _expert_call(l, tb, xs, w_gu, b_gu, w_dn, b_dn):
    return pl.pallas_call(
        functools.partial(_expert_kernel, layer=l),
        grid_spec=pltpu.PrefetchScalarGridSpec(
            num_scalar_prefetch=2,
            grid=(N_EXP,),
            in_specs=[
                pl.BlockSpec((None, None, 1, 2 * FF), lambda e, *_: (l, e, 0, 0)),
                pl.BlockSpec((None, None, 1, D), lambda e, *_: (l, e, 0, 0)),
                pl.BlockSpec(memory_space=pl.ANY),
                pl.BlockSpec(memory_space=pl.ANY),
                pl.BlockSpec(memory_space=pl.ANY),
            ],
            out_specs=pl.BlockSpec(memory_space=pl.ANY),
            scratch_shapes=[pltpu.VMEM((2, D, 2 * FF), F32), pltpu.VMEM((2, FF, D), F32),
                            pltpu.VMEM((D, 2 * FF), BF16), pltpu.VMEM((FF, D), BF16),
                            pltpu.VMEM((2, ET, XS_W), F32), pltpu.VMEM((2, ET, D), F32),
                            pltpu.SemaphoreType.DMA((2, 2)), pltpu.SemaphoreType.DMA((2,)),
                            pltpu.SemaphoreType.DMA((2,))]),
        out_shape=jax.ShapeDtypeStruct((N_SLOTS, D), F32),
        compiler_params=_cparams(("arbitrary",)),
        name="moe_experts",
    )(tb["trow"], tb["gstart"], b_gu.reshape(DEPTH, N_EXP, 1, 2 * FF), b_dn.reshape(DEPTH, N_EXP, 1, D),
      xs, w_gu, w_dn)


def _combine_kernel(src_ref, dst_ref, npc_ref, gt_ref, lo_ref, hi_ref, x1_ref, mod_ref, ys_ref, o_ref,
                    buf_ref, sem_ref):
    i = pl.program_id(0)
    slot = i & 1

    def piece_copy(tile, p, s):
        src = src_ref[tile * N_PIECE_MAX + p]
        dst = dst_ref[tile * N_PIECE_MAX + p]
        return pltpu.make_async_copy(ys_ref.at[pl.ds(pl.multiple_of(dst, SUB), PIECE)],
                                     buf_ref.at[s, pl.ds(pl.multiple_of(src, SUB), PIECE)], sem_ref.at[s])

    @pl.when(i == 0)
    def _():
        buf_ref[...] = jnp.zeros(buf_ref.shape, F32)
        _piece_loop(npc_ref[0], lambda p: piece_copy(0, p, 0).start())

    @pl.when(i + 1 < pl.num_programs(0))
    def _():
        _piece_loop(npc_ref[i + 1], lambda p: piece_copy(i + 1, p, 1 - slot).start())

    rank_t = _rank_in_tile(gt_ref[...])
    r = lax.broadcasted_iota(I32, (TM, TM), 0)
    c = lax.broadcasted_iota(I32, (TM, TM), 1)
    eye = (r == c).astype(BF16)
    rk = _dot_nt(eye, rank_t.astype(BF16)).astype(BF16)
    lo = lo_ref[:, 0:1]
    hi = hi_ref[:, 0:1]

    _piece_loop(npc_ref[i], lambda p: piece_copy(i, p, slot).wait())

    def chunk(cidx, acc):
        rr = (lax.broadcasted_iota(I32, (N_EXP, TM), 1) + cidx * TM).astype(F32)
        own = jnp.logical_and(rr >= lo, rr < hi)
        base = jnp.sum(jnp.where(own, lo, 0.0), axis=0, keepdims=True)
        owned = jnp.sum(jnp.where(own, 1.0, 0.0), axis=0, keepdims=True) > 0.0
        in_run = jnp.where(owned, rr[0:1, :] - base, -1.0)
        hit = _dot(rk, own.astype(BF16)) == in_run
        y = buf_ref[slot, cidx * TM:(cidx + 1) * TM, :].astype(BF16)
        return acc + _dot(hit.astype(BF16), y)

    acc = jnp.zeros((TM, D), F32)
    for cidx in range(UNSORT_ROWS // TM):
        acc = chunk(cidx, acc)
    o_ref[...] = x1_ref[...] + mod_ref[...][:, 5 * D:6 * D] * acc


def _combine_call(l, tb, gate_t, x1, mods, ys):
    tab = pl.BlockSpec((None, N_EXP, LANES), lambda i, *_: (i, 0, 0))
    return pl.pallas_call(
        _combine_kernel,
        grid_spec=pltpu.PrefetchScalarGridSpec(
            num_scalar_prefetch=3,
            grid=(N_TILES,),
            in_specs=[
                pl.BlockSpec((N_EXP, TM), lambda i, *_: (0, i)), tab, tab,
                pl.BlockSpec((TM, D), lambda i, *_: (i, 0)),
                pl.BlockSpec((None, 1, 6 * D), lambda i, *_: (l * 8 + _grp(i), 0, 0)),
                pl.BlockSpec(memory_space=pl.ANY),
            ],
            out_specs=pl.BlockSpec((TM, D), lambda i, *_: (i, 0)),
            scratch_shapes=[pltpu.VMEM((2, UNSORT_ROWS, D), F32), pltpu.SemaphoreType.DMA((2,))]),
        out_shape=jax.ShapeDtypeStruct((T, D), F32),
        compiler_params=_cparams(("arbitrary",)),
        name="moe_combine",
    )(tb["src_c"], tb["dst"], tb["n_piece"], gate_t, tb["lo_col"], tb["hi_col"], x1, mods, ys)


def _rope_tables():
    t = jnp.arange(S_LAT, dtype=I32)
    row = (t // GRID_W).astype(F32)
    col = (t % GRID_W).astype(F32)

    def table(h, reps):
        inv = ROPE_THETA ** (-jnp.arange(0, h, 2, dtype=F32) / h)
        ar = row[:, None] * inv[None, :]
        ac = col[:, None] * inv[None, :]
        cos = jnp.concatenate([jnp.cos(ar), jnp.cos(ar), jnp.cos(ac), jnp.cos(ac)], axis=-1)
        sin = jnp.concatenate([-jnp.sin(ar), jnp.sin(ar), -jnp.sin(ac), jnp.sin(ac)], axis=-1)
        cos = jnp.tile(cos, (1, reps))
        sin = jnp.tile(sin, (1, reps))
        ident_c = jnp.ones((TM, LANES), F32)
        ident_s = jnp.zeros((TM, LANES), F32)
        return jnp.concatenate([ident_c, cos], axis=0), jnp.concatenate([ident_s, sin], axis=0)

    c64, s64 = table(HD // 2, 2)
    c32, s32 = table(MLA_ROPE // 2, 4)
    return c64, s64, c32, s32


def _group_matrix(n_rows, row_group, n_cols, col_group):
    r = np.arange(n_rows)[:, None] // row_group
    c = np.arange(n_cols)[None, :] // col_group
    return jnp.asarray((r == c).astype(np.float32), dtype=BF16)


def kernel(x_prompt, x_sample, cache_gqa_k, cache_gqa_v, cache_mla_ckv, cache_mla_krope, c, c_ctx,
           w_ada, b_ada, norm1_g, w_in, gqa_qn_g, gqa_kn_g, w_o_gqa, mla_cq_g, w_uq, mla_ckv_g, w_ukv,
           mla_qn_g, mla_kn_g, mla_krn_g, w_o_mla, conv_dw, conv_dw_b, conv_ln_g, conv_ln_b, w_pw2,
           w_out, norm2_g, w_router, b_router, w_gu, b_gu, w_dn, b_dn):
    L = DEPTH
    c64, s64, c32, s32 = _rope_tables()
    consts = dict(c64=c64, s64=s64, c32=c32, s32=s32,
                  b64=_group_matrix(512, HD, 512, HD), brn=_group_matrix(256, MLA_ROPE, 512, HD),
                  bnr=_group_matrix(512, HD, 256, MLA_ROPE),
                  brr=_group_matrix(256, MLA_ROPE, 256, MLA_ROPE))

    pad = jnp.zeros((L, D, LANES - MLA_ROPE), F32)
    wa = jnp.concatenate([w_in[:, :, 0:1440], pad, w_in[:, :, 1440:2464]], axis=-1).astype(BF16)
    wg = w_in[:, :, 2464:].astype(BF16)
    uq = w_uq.reshape(L, MLA_Q_RANK, NH, MLA_QK)
    uq = jnp.concatenate([uq[..., :HD].reshape(L, MLA_Q_RANK, NH * HD),
                          uq[..., HD:].reshape(L, MLA_Q_RANK, NH * MLA_ROPE)], axis=-1).astype(BF16)
    ukv = w_ukv.reshape(L, MLA_KV_RANK, NH, 2 * HD)
    ukv = jnp.concatenate([ukv[..., :HD].reshape(L, MLA_KV_RANK, NH * HD),
                           ukv[..., HD:].reshape(L, MLA_KV_RANK, NH * HD)], axis=-1).astype(BF16)

    def row3(a):
        return a.reshape(L, 1, a.shape[-1])

    wr_t = jnp.swapaxes(w_router, 1, 2)
    wr_hi = wr_t.astype(BF16)
    lw = dict(
        norm1_g=row3(norm1_g), wa=wa, wg=wg, w_uq=uq, w_ukv=ukv,
        gqa_qn_g=row3(jnp.tile(gqa_qn_g, (1, NH))), gqa_kn_g=row3(jnp.tile(gqa_kn_g, (1, GQA_KV))),
        mla_cq_g=row3(mla_cq_g), mla_ckv_g=row3(mla_ckv_g),
        mla_krn_g=row3(jnp.concatenate([mla_krn_g, jnp.zeros((L, LANES - MLA_ROPE), F32)], axis=-1)),
        mla_qn_n=row3(jnp.tile(mla_qn_g[:, :HD], (1, NH))),
        mla_qn_r=row3(jnp.tile(mla_qn_g[:, HD:], (1, NH))),
        mla_kn_g=row3(jnp.tile(mla_kn_g, (1, NH))),
        conv_dw=conv_dw, conv_dw_b=row3(conv_dw_b), conv_ln_g=row3(conv_ln_g), conv_ln_b=row3(conv_ln_b),
        w_o_gqa=w_o_gqa.astype(BF16), w_o_mla=w_o_mla.astype(BF16), w_pw2=w_pw2.astype(BF16),
        w_out=w_out.astype(BF16), norm2_g=row3(norm2_g),
        wr_hi=wr_hi, wr_lo=(wr_t - wr_hi.astype(F32)).astype(BF16),
        b_router=b_router.reshape(L, N_EXP, 1),
    )

    cvec = jnp.concatenate([c_ctx[None, :], c, jnp.zeros((8 - 1 - N_LAT_B, D), F32)], axis=0)
    mods = _ada_call(cvec, w_ada, b_ada).reshape(L * 8, 1, 6 * D)

    ck = cache_gqa_k.reshape(N_LAT_B, L, PAST, GQA_KV * HD)
    cv_ = cache_gqa_v.reshape(N_LAT_B, L, PAST, GQA_KV * HD)
    kr_placed = jnp.pad(cache_mla_krope, ((0, 0), (0, 0), (0, 0), (HD, LANES - MLA_QK)))
    km_c, vm_c = _cache_kv_call(cache_mla_ckv, kr_placed, ukv, lw["mla_kn_g"], consts["b64"])

    x = jnp.concatenate([x_prompt.reshape(T_CTX, D), x_sample.reshape(T_LAT, D)], axis=0)
    new_k, new_v, new_ckv, new_kr = [], [], [], []
    for l in range(L):
        qg, kg, vg, qm, ckv, kr, km, vm, u = _inproj_call(l, x, mods, lw, consts)
        new_k.append(kg[:T_CTX].reshape(N_CTX_B, S_CTX, GQA_KV, HD))
        new_v.append(vg[:T_CTX].reshape(N_CTX_B, S_CTX, GQA_KV, HD))
        new_ckv.append(ckv[:T_CTX].reshape(N_CTX_B, S_CTX, MLA_KV_RANK))
        new_kr.append(kr[:T_CTX, :MLA_ROPE].reshape(N_CTX_B, S_CTX, MLA_ROPE))
        ag_c = _attn_call(l, False, False, qg, kg, vg, None, None)
        ag_l = _attn_call(l, True, False, qg, kg, vg, ck, cv_)
        am_c = _attn_call(l, False, True, qm, km, vm, None, None)
        am_l = _attn_call(l, True, True, qm, km, vm, km_c, vm_c)
        x1, h2, gate_t, cnt = _merge_call(l, x, mods, lw, ag_c, ag_l, am_c, am_l, u)
        tb = _moe_tables(cnt[:, :, 0])
        xs = _dispatch_call(tb, gate_t, h2)
        ys = _expert_call(l, tb, xs, w_gu, b_gu, w_dn, b_dn)
        x = _combine_call(l, tb, gate_t, x1, mods, ys)

    return (x[:T_CTX].reshape(N_CTX_B, S_CTX, D), x[T_CTX:].reshape(N_LAT_B, S_LAT, D),
            jnp.stack(new_k, axis=1), jnp.stack(new_v, axis=1),
            jnp.stack(new_ckv, axis=1), jnp.stack(new_kr, axis=1))
```

```python
import functools

import numpy as np
import jax
import jax.numpy as jnp
from jax import lax
from jax.experimental import pallas as pl
from jax.experimental.pallas import tpu as pltpu

F32 = jnp.float32
BF16 = jnp.bfloat16
I32 = jnp.int32

D = 1024
DEPTH = 4
N_CTX_B, S_CTX = 16, 256
N_LAT_B, S_LAT = 2, 2048
PAST = 256
GRID_W = 64
ROPE_THETA = 10000.0
EPS = 1e-6
HD = 64
NH = 8
GQA_KV = 2
GQA_GROUP = NH // GQA_KV
MLA_Q_RANK = 384
MLA_KV_RANK = 256
MLA_ROPE = 32
MLA_QK = HD + MLA_ROPE
CONV_CH = 512
CONV_K = 31
CONV_PAD = CONV_K // 2
N_EXP = 32
TOP_K = 4
FF = 1024
SWIGLU_LIMIT = 7.0
SWIGLU_ALPHA = 1.702

T_CTX = N_CTX_B * S_CTX
T_LAT = N_LAT_B * S_LAT
T = T_CTX + T_LAT

LANES = 128
TM = 256
N_TILES = T // TM
CTX_TILES = T_CTX // TM
LAT_TILES_PER_B = S_LAT // TM
CTX_SEQ_PER_STEP = 1
TILES_PER_STEP = 2
WA = 2560
GATES_W = 3 * D
SUB = 8
PIECE = 32
ET = 256
XS_W = D + LANES
SORT_ROWS = 1280
SORT_CHUNKS = SORT_ROWS // TM
assert TM * TOP_K + (SUB - 1) * N_EXP + PIECE - SUB <= SORT_ROWS
UNSORT_ROWS = 2048
assert TM * TOP_K + (PIECE - 1) * N_EXP <= UNSORT_ROWS
N_PIECE_MAX = (TM * TOP_K + (SUB - 1) * N_EXP) // PIECE + N_EXP
FLUSH = ET + PIECE
SEG_SLACK = 2 * ET
N_ET_MAX = (T * TOP_K + (SUB - 1) * N_TILES * N_EXP + N_EXP * (PIECE - SUB)) // ET + N_EXP
N_SLOTS = N_ET_MAX * ET + N_EXP * SEG_SLACK
VMEM_LIMIT = 56 * 1024 * 1024


def _dot(a, b):
    return jnp.dot(a, b, preferred_element_type=F32)


def _dot_nt(a, b):
    return lax.dot_general(a, b, (((1,), (1,)), ((), ())), preferred_element_type=F32)


def _split3(x):
    hi = x.astype(BF16)
    r = x - hi.astype(F32)
    mid = r.astype(BF16)
    lo = (r - mid.astype(F32)).astype(BF16)
    return hi, mid, lo


def _cparams(sem, vmem=VMEM_LIMIT):
    return pltpu.CompilerParams(dimension_semantics=sem, vmem_limit_bytes=vmem)


def _grp(i):
    return jnp.where(i < CTX_TILES, 0, 1 + (i - CTX_TILES) // LAT_TILES_PER_B)


def _pos_blk(i):
    return jnp.where(i < CTX_TILES, 0, 1 + (i - CTX_TILES) % LAT_TILES_PER_B)


def _ada_kernel(c_ref, w_ref, b_ref, o_ref):
    c = c_ref[...]
    s = c * jax.nn.sigmoid(c)
    w = w_ref[...]
    s_hi, s_lo, _ = _split3(s)
    w_hi, w_lo, _ = _split3(w)
    o_ref[...] = _dot(s_hi, w_hi) + _dot(s_hi, w_lo) + _dot(s_lo, w_hi) + b_ref[...]


def _ada_call(cvec, w_ada, b_ada):
    tn = 1536
    return pl.pallas_call(
        _ada_kernel,
        grid=(DEPTH, 6 * D // tn),
        in_specs=[
            pl.BlockSpec((8, D), lambda l, j: (0, 0)),
            pl.BlockSpec((None, D, tn), lambda l, j: (l, 0, j)),
            pl.BlockSpec((None, 1, tn), lambda l, j: (l, 0, j)),
        ],
        out_specs=pl.BlockSpec((None, 8, tn), lambda l, j: (l, 0, j)),
        out_shape=jax.ShapeDtypeStruct((DEPTH, 8, 6 * D), F32),
        compiler_params=_cparams(("arbitrary", "arbitrary")),
        name="ada_mod",
    )(cvec, w_ada, b_ada.reshape(DEPTH, 1, 6 * D))


def _norm_mod(x, g, scale, shift):
    ms = jnp.mean(x * x, axis=-1, keepdims=True)
    return (x * lax.rsqrt(ms + EPS) * g) * (1.0 + scale) + shift


def _group_rms(x, ss, n, g):
    return x * lax.rsqrt(ss * (1.0 / n) + EPS) * g


def _partner(x, dist):
    n = x.shape[-1]
    lane = lax.broadcasted_iota(I32, x.shape, 1)
    r1 = pltpu.roll(x, dist, 1)
    r2 = pltpu.roll(x, n - dist, 1)
    r1_is_minus = pltpu.roll(lane, dist, 1) == ((lane + (n - dist)) & (n - 1))
    plus = jnp.where(r1_is_minus, r2, r1)
    minus = jnp.where(r1_is_minus, r1, r2)
    return jnp.where((lane & (2 * dist - 1)) < dist, plus, minus)


def _rope(x, cos, sin_signed, dist):
    return x * cos + _partner(x, dist) * sin_signed


def _tile_lanes(x, reps):
    return jnp.concatenate([x] * reps, axis=1)


def _lanes_up(x, s):
    lane = lax.broadcasted_iota(I32, x.shape, 1)
    r1 = pltpu.roll(x, s, 1)
    r2 = pltpu.roll(x, LANES - s, 1)
    r1_moves_up = pltpu.roll(lane, s, 1) == ((lane + (LANES - s)) & (LANES - 1))
    return jnp.where(r1_moves_up, r1, r2)


def _mla_head_blocks(nope, rope_blocks):
    lane = lax.broadcasted_iota(I32, (nope.shape[0], LANES), 1)
    blocks = []
    for h in range(NH):
        pair = nope[:, LANES * (h // 2):LANES * (h // 2) + LANES]
        if h % 2 == 1:
            pair = pltpu.roll(pair, HD, 1)
        blocks.append(jnp.where(lane < HD, pair, jnp.where(lane < MLA_QK, rope_blocks[h], 0.0)))
    return jnp.concatenate(blocks, axis=1).astype(BF16)


def _inproj_kernel(x_ref, mod_ref, g1_ref, wa_ref, wuq_ref, wukv_ref,
                   qng_ref, kng_ref, cqg_ref, ckvg_ref, krg_ref, mqn_ref, mqr_ref, mkn_ref,
                   c64_ref, s64_ref, c32_ref, s32_ref,
                   b64_ref, brn_ref, bnr_ref, brr_ref,
                   qg_ref, kg_ref, vg_ref, qm_ref, ckv_ref, kr_ref, km_ref, vm_ref, u_ref):
    mod = mod_ref[...]
    h = _norm_mod(x_ref[...], g1_ref[...], mod[:, D:2 * D], mod[:, 0:D])
    p = _dot(h.astype(BF16), wa_ref[...])
    b64 = b64_ref[...]
    c64 = c64_ref[...]
    s64 = s64_ref[...]
    c32 = c32_ref[...]
    s32 = s32_ref[...]

    q = p[:, 0:512]
    q = _group_rms(q, _dot((q * q).astype(BF16), b64), HD, qng_ref[...])
    q = _rope(q, _tile_lanes(c64, 4), _tile_lanes(s64, 4), 16)
    qg_ref[...] = (q * (HD ** -0.5)).astype(BF16)
    k = p[:, 512:640]
    k = _group_rms(k, _dot((k * k).astype(BF16), b64[0:128, 0:128]), HD, kng_ref[...])
    kg_ref[...] = _rope(k, c64, s64, 16)
    vg_ref[...] = p[:, 640:768]

    cq = p[:, 768:1152]
    cq = cq * lax.rsqrt(jnp.mean(cq * cq, axis=-1, keepdims=True) + EPS) * cqg_ref[...]
    qm = _dot(cq.astype(BF16), wuq_ref[...])
    qmn = qm[:, 0:512]
    qmr = qm[:, 512:768]
    sqn = (qmn * qmn).astype(BF16)
    sqr = (qmr * qmr).astype(BF16)
    ss_n = _dot(sqn, b64) + _dot(sqr, brn_ref[...])
    ss_r = _dot(sqn, bnr_ref[...]) + _dot(sqr, brr_ref[...])
    scale = MLA_QK ** -0.5
    qmn = _group_rms(qmn, ss_n, MLA_QK, mqn_ref[...]) * scale
    qmr = _group_rms(qmr, ss_r, MLA_QK, mqr_ref[...])
    qmr = _rope(qmr, _tile_lanes(c32, 2), _tile_lanes(s32, 2), 8) * scale
    q_rope = []
    for hd in range(NH):
        blk = qmr[:, LANES * (hd // 4):LANES * (hd // 4) + LANES]
        up = (HD - MLA_ROPE * (hd % 4)) % LANES
        q_rope.append(_lanes_up(blk, up) if up else blk)
    qm_ref[...] = _mla_head_blocks(qmn, q_rope)

    kr = p[:, 1408:1536]
    kr = kr * lax.rsqrt(jnp.sum(kr * kr, axis=-1, keepdims=True) * (1.0 / MLA_ROPE) + EPS) * krg_ref[...]
    kr = _rope(kr, c32, s32, 8)
    kr_ref[...] = kr

    ckv = p[:, 1152:1408]
    ckv = ckv * lax.rsqrt(jnp.mean(ckv * ckv, axis=-1, keepdims=True) + EPS) * ckvg_ref[...]
    ckv_ref[...] = ckv
    kv = _dot(ckv.astype(BF16), wukv_ref[...])
    kn = kv[:, 0:512]
    kn = _group_rms(kn, _dot((kn * kn).astype(BF16), b64), HD, mkn_ref[...])
    km_ref[...] = _mla_head_blocks(kn, [pltpu.roll(kr, HD, 1)] * NH)
    vm_ref[...] = kv[:, 512:1024].astype(BF16)

    u_ref[...] = p[:, 1536:2048] * jax.nn.sigmoid(p[:, 2048:2560])


def _inproj_call(l, x, mods, lw, consts):
    def full(a):
        return pl.BlockSpec(a.shape, lambda i: (0,) * a.ndim)

    def layer(a):
        return pl.BlockSpec((None,) + a.shape[1:], lambda i: (l,) + (0,) * (a.ndim - 1))

    def rows(w):
        return pl.BlockSpec((TM, w), lambda i: (i, 0))

    def tab():
        return pl.BlockSpec((TM, LANES), lambda i: (_pos_blk(i), 0))

    layer_ws = [lw["norm1_g"], lw["wa"], lw["w_uq"], lw["w_ukv"], lw["gqa_qn_g"], lw["gqa_kn_g"],
                lw["mla_cq_g"], lw["mla_ckv_g"], lw["mla_krn_g"], lw["mla_qn_n"], lw["mla_qn_r"],
                lw["mla_kn_g"]]
    tabs = [consts["c64"], consts["s64"], consts["c32"], consts["s32"]]
    bms = [consts["b64"], consts["brn"], consts["bnr"], consts["brr"]]
    out_w = [(512, BF16), (128, F32), (128, F32), (NH * LANES, BF16), (256, F32), (128, F32),
             (NH * LANES, BF16), (512, BF16), (512, F32)]
    return pl.pallas_call(
        _inproj_kernel,
        grid=(N_TILES,),
        in_specs=[rows(D), pl.BlockSpec((None, 1, 6 * D), lambda i: (l * 8 + _grp(i), 0, 0))]
        + [layer(a) for a in layer_ws] + [tab() for _ in tabs] + [full(a) for a in bms],
        out_specs=[rows(w) for w, _ in out_w],
        out_shape=[jax.ShapeDtypeStruct((T, w), dt) for w, dt in out_w],
        compiler_params=_cparams(("arbitrary",)),
        name="inproj",
    )(x, mods, *layer_ws, *tabs, *bms)


def _cache_kv_kernel(ckv_ref, kr_ref, wukv_ref, mkn_ref, b64_ref, km_ref, vm_ref):
    kv = _dot(ckv_ref[...].astype(BF16), wukv_ref[...])
    kn = kv[:, 0:512]
    kn = _group_rms(kn, _dot((kn * kn).astype(BF16), b64_ref[...]), HD, mkn_ref[...])
    km_ref[...] = _mla_head_blocks(kn, [kr_ref[...]] * NH)
    vm_ref[...] = kv[:, 512:1024].astype(BF16)


def _cache_kv_call(cache_ckv, cache_kr_placed, w_ukv, mla_kn_g, b64):
    def out(w):
        return pl.BlockSpec((None, None, PAST, w), lambda b, l: (b, l, 0, 0))

    return pl.pallas_call(
        _cache_kv_kernel,
        grid=(N_LAT_B, DEPTH),
        in_specs=[
            out(MLA_KV_RANK), out(LANES),
            pl.BlockSpec((None, MLA_KV_RANK, 1024), lambda b, l: (l, 0, 0)),
            pl.BlockSpec((None, 1, 512), lambda b, l: (l, 0, 0)),
            pl.BlockSpec((512, 512), lambda b, l: (0, 0)),
        ],
        out_specs=[out(NH * LANES), out(512)],
        out_shape=[jax.ShapeDtypeStruct((N_LAT_B, DEPTH, PAST, NH * LANES), BF16),
                   jax.ShapeDtypeStruct((N_LAT_B, DEPTH, PAST, 512), BF16)],
        compiler_params=_cparams(("arbitrary", "arbitrary")),
        name="cache_kv",
    )(cache_ckv, cache_kr_placed, w_ukv, mla_kn_g, b64)


def _attn_kernel(*refs, mla, has_cache, n_seq):
    if has_cache:
        q_ref, ks_ref, vs_ref, kc_ref, vc_ref, o_ref = refs
    else:
        q_ref, ks_ref, vs_ref, o_ref = refs
    qk_w = LANES if mla else HD
    n_k = ks_ref.shape[0] // n_seq

    def v_with_ones(v_ref, rows, blk, half):
        v = v_ref[rows, LANES * blk:LANES * blk + LANES].astype(BF16)
        lane = lax.broadcasted_iota(I32, v.shape, 1)
        return jnp.where(lane >= HD if half == 1 else lane < HD, v, jnp.ones_like(v))

    halves = [[] for _ in range(n_seq)]
    for j in range(n_seq):
        qrows = slice(j * TM, (j + 1) * TM)
        krows = slice(j * n_k, (j + 1) * n_k)
        if not has_cache:
            outs = []
            for h in range(NH):
                kvh = h if mla else h // GQA_GROUP
                s = _dot_nt(q_ref[qrows, qk_w * h:qk_w * h + qk_w],
                            ks_ref[krows, qk_w * kvh:qk_w * kvh + qk_w].astype(BF16))
                p = jnp.exp(s - jnp.max(s, axis=-1, keepdims=True))
                den = jnp.sum(p, axis=-1, keepdims=True)
                o = _dot(p.astype(BF16), vs_ref[krows, HD * kvh:HD * kvh + HD].astype(BF16))
                outs.append((o / den).astype(BF16))
            o_ref[qrows, :] = jnp.concatenate(outs, axis=1)
            continue
        for h in range(NH):
            kvh = h if mla else h // GQA_GROUP
            blk, half = kvh // 2, kvh % 2
            sl = slice(qk_w * kvh, qk_w * kvh + qk_w)
            qh = q_ref[qrows, qk_w * h:qk_w * h + qk_w]
            s = _dot_nt(qh, ks_ref[krows, sl].astype(BF16))
            m = jnp.max(s, axis=-1, keepdims=True)
            if has_cache:
                sc = _dot_nt(qh, kc_ref[:, sl].astype(BF16))
                m = jnp.maximum(m, jnp.max(sc, axis=-1, keepdims=True))
            o = _dot(jnp.exp((s - m).astype(BF16)), v_with_ones(vs_ref, krows, blk, half))
            if has_cache:
                o = o + _dot(jnp.exp((sc - m).astype(BF16)), v_with_ones(vc_ref, slice(None), blk, half))
            den = o[:, 0:1] if half == 1 else o[:, HD:HD + 1]
            on = o / den
            if half != h % 2:
                on = pltpu.roll(on, HD, 1)
            halves[j].append(on)
    lane = lax.broadcasted_iota(I32, (TM, LANES), 1)
    for j in range(n_seq if has_cache else 0):
        o_ref[j * TM:(j + 1) * TM, :] = jnp.concatenate(
            [jnp.where(lane < HD, halves[j][2 * b], halves[j][2 * b + 1]).astype(BF16)
             for b in range(NH // 2)], axis=1)


def _attn_call(l, latent, mla, q, ks, vs, kc, vc):
    if latent:
        nb, nq, n_seq, n_self, q0, k0 = N_LAT_B, S_LAT // TM, 1, S_LAT, CTX_TILES, T_CTX // S_LAT
    else:
        n_seq = CTX_SEQ_PER_STEP
        nb, nq, n_self, q0, k0 = N_CTX_B // n_seq, 1, n_seq * S_CTX, 0, 0
    tq = n_seq * TM

    def qspec(w):
        return pl.BlockSpec((tq, w), lambda b, i: (q0 + b * nq + i, 0))

    def kspec(w):
        return pl.BlockSpec((n_self, w), lambda b, i: (k0 + b, 0))

    def cspec(w):
        return pl.BlockSpec((None, None, PAST, w), lambda b, i: (b, l, 0, 0))

    args = [q, ks, vs]
    specs = [qspec(q.shape[1]), kspec(ks.shape[1]), kspec(vs.shape[1])]
    if latent:
        args += [kc, vc]
        specs += [cspec(ks.shape[1]), cspec(vs.shape[1])]
    return pl.pallas_call(
        functools.partial(_attn_kernel, mla=mla, has_cache=latent, n_seq=n_seq),
        grid=(nb, nq),
        in_specs=specs,
        out_specs=pl.BlockSpec((tq, 512), lambda b, i: (b * nq + i, 0)),
        out_shape=jax.ShapeDtypeStruct((nb * nq * tq, 512), BF16),
        compiler_params=_cparams(("arbitrary", "arbitrary")),
        name=("mla" if mla else "gqa") + ("_lat" if latent else "_ctx"),
    )(*args)


def _conv_tile(tile, prev16, cur, next16, dw_ref, dwb_ref, lng_ref, lnb_ref, win_ref, rot_ref):
    in_seq = (tile - CTX_TILES) % LAT_TILES_PER_B
    has_prev = jnp.logical_and(tile >= CTX_TILES, in_seq != 0)
    has_next = jnp.logical_and(tile >= CTX_TILES, in_seq != LAT_TILES_PER_B - 1)
    win_ref[0:16, :] = jnp.where(has_prev, prev16, 0.0)
    win_ref[16:16 + TM, :] = cur
    win_ref[16 + TM:32 + TM, :] = jnp.where(has_next, next16, 0.0)
    rot_rows = rot_ref.shape[1]
    for sh in range(1, SUB):
        rot_ref[sh - 1] = win_ref[sh:sh + rot_rows, :]
    acc = jnp.zeros((TM, CONV_CH), F32) + dwb_ref[...]
    for j in range(CONV_K):
        al, sh = (j + 1) // SUB * SUB, (j + 1) % SUB
        tap = win_ref[al:al + TM, :] if sh == 0 else rot_ref[sh - 1, al:al + TM, :]
        acc = acc + tap * dw_ref[j:j + 1, :]
    mu = jnp.mean(acc, axis=-1, keepdims=True)
    cen = acc - mu
    var = jnp.mean(cen * cen, axis=-1, keepdims=True)
    y = cen * lax.rsqrt(var + EPS) * lng_ref[...] + lnb_ref[...]
    return (y * jax.nn.sigmoid(y)).astype(BF16)


def _merge_kernel(x_ref, mod_ref, g1_ref, wg_ref, agc_ref, agl_ref, amc_ref, aml_ref,
                  up_ref, uc_ref, un_ref, dw_ref, dwb_ref, lng_ref, lnb_ref,
                  wog_ref, wom_ref, wpw_ref, wout_ref, g2_ref, wr_hi_ref, wr_lo_ref, br_ref,
                  x1_ref, h2_ref, gt_ref, cnt_ref, win_ref, rot_ref):
    step = pl.program_id(0)
    is_ctx = step < CTX_TILES // TILES_PER_STEP
    mod = mod_ref[...]
    last = TILES_PER_STEP - 1
    for j in range(TILES_PER_STEP):
        rows = slice(j * TM, (j + 1) * TM)
        prev16 = up_ref[TM - 16:TM, :] if j == 0 else uc_ref[j * TM - 16:j * TM, :]
        next16 = un_ref[0:16, :] if j == last else uc_ref[(j + 1) * TM:(j + 1) * TM + 16, :]
        cv = _conv_tile(step * TILES_PER_STEP + j, prev16, uc_ref[rows, :], next16,
                        dw_ref, dwb_ref, lng_ref, lnb_ref, win_ref.at[j], rot_ref.at[j])
        _merge_tile(rows, j, is_ctx, mod, cv, x_ref, g1_ref, wg_ref, agc_ref, agl_ref,
                    amc_ref, aml_ref, wog_ref, wom_ref, wpw_ref, wout_ref, g2_ref, wr_hi_ref,
                    wr_lo_ref, br_ref, x1_ref, h2_ref, gt_ref, cnt_ref)


def _merge_tile(rows, j, is_ctx, mod, cv, x_ref, g1_ref, wg_ref, agc_ref, agl_ref, amc_ref, aml_ref,
                wog_ref, wom_ref, wpw_ref, wout_ref, g2_ref, wr_hi_ref, wr_lo_ref, br_ref,
                x1_ref, h2_ref, gt_ref, cnt_ref):
    x = x_ref[rows, :]
    h = _norm_mod(x, g1_ref[...], mod[:, D:2 * D], mod[:, 0:D]).astype(BF16)
    gates = jax.nn.sigmoid(_dot(h, wg_ref[...]))
    ag = jnp.where(is_ctx, agc_ref[rows, :], agl_ref[rows, :])
    am = jnp.where(is_ctx, amc_ref[rows, :], aml_ref[rows, :])
    merged = (gates[:, 0:D] * _dot(ag, wog_ref[...])
              + gates[:, D:2 * D] * _dot(am, wom_ref[...])
              + gates[:, 2 * D:3 * D] * _dot(cv, wpw_ref[...]))
    x1 = x + mod[:, 2 * D:3 * D] * _dot(merged.astype(BF16), wout_ref[...])
    x1_ref[rows, :] = x1
    h2 = _norm_mod(x1, g2_ref[...], mod[:, 4 * D:5 * D], mod[:, 3 * D:4 * D])
    h2_hi = h2.astype(BF16)
    h2_ref[rows, :] = h2_hi
    h2_lo = (h2 - h2_hi.astype(F32)).astype(BF16)
    wr_hi = wr_hi_ref[...]
    logit = _dot_nt(wr_hi, h2_hi) + _dot_nt(wr_hi, h2_lo) + _dot_nt(wr_lo_ref[...], h2_hi) + br_ref[...]
    row = lax.broadcasted_iota(I32, logit.shape, 0).astype(F32)
    sels, vals = [], []
    for _ in range(TOP_K):
        m = jnp.max(logit, axis=0, keepdims=True)
        idx = jnp.min(jnp.where(logit == m, row, float(N_EXP)), axis=0, keepdims=True)
        sel = row == idx
        sels.append(sel)
        vals.append(m)
        logit = jnp.where(sel, -jnp.inf, logit)
    es = [jnp.exp(v - vals[0]) for v in vals]
    den = es[0] + es[1] + es[2] + es[3]
    gate = jnp.zeros(logit.shape, F32)
    for sel, e in zip(sels, es):
        gate = jnp.where(sel, e / den, gate)
    gt_ref[:, rows] = gate
    n = jnp.sum((gate > 0.0).astype(F32), axis=1, keepdims=True)
    cnt_ref[j] = jnp.broadcast_to(n, (N_EXP, LANES)).astype(I32)


def _merge_call(l, x, mods, lw, ag_c, ag_l, am_c, am_l, u):
    tps = TILES_PER_STEP
    tr = tps * TM

    def layer(a):
        return pl.BlockSpec((None,) + a.shape[1:], lambda i: (l,) + (0,) * (a.ndim - 1))

    def rows(w):
        return pl.BlockSpec((tr, w), lambda i: (i, 0))

    ctx_rows = pl.BlockSpec((tr, 512), lambda i: (jnp.minimum(i, CTX_TILES // tps - 1), 0))
    lat_rows = pl.BlockSpec((tr, 512), lambda i: (jnp.maximum(i - CTX_TILES // tps, 0), 0))
    u_prev = pl.BlockSpec((TM, CONV_CH), lambda i: (jnp.maximum(i * tps - 1, 0), 0))
    u_next = pl.BlockSpec((TM, CONV_CH), lambda i: (jnp.minimum((i + 1) * tps, N_TILES - 1), 0))
    ws1 = [lw["norm1_g"], lw["wg"]]
    wsc = [lw["conv_dw"], lw["conv_dw_b"], lw["conv_ln_g"], lw["conv_ln_b"]]
    ws2 = [lw["w_o_gqa"], lw["w_o_mla"], lw["w_pw2"], lw["w_out"], lw["norm2_g"],
           lw["wr_hi"], lw["wr_lo"], lw["b_router"]]
    return pl.pallas_call(
        _merge_kernel,
        grid=(N_TILES // tps,),
        in_specs=[rows(D), pl.BlockSpec((None, 1, 6 * D), lambda i: (l * 8 + _grp(i * tps), 0, 0))]
        + [layer(a) for a in ws1] + [ctx_rows, lat_rows, ctx_rows, lat_rows, u_prev, rows(CONV_CH), u_next]
        + [layer(a) for a in wsc] + [layer(a) for a in ws2],
        scratch_shapes=[pltpu.VMEM((tps, TM + 32, CONV_CH), F32),
                        pltpu.VMEM((tps, SUB - 1, TM + 24, CONV_CH), F32)],
        out_specs=[rows(D), rows(D), pl.BlockSpec((N_EXP, tr), lambda i: (0, i)),
                   pl.BlockSpec((tps, N_EXP, LANES), lambda i: (i, 0, 0))],
        out_shape=[jax.ShapeDtypeStruct((T, D), F32), jax.ShapeDtypeStruct((T, D), BF16),
                   jax.ShapeDtypeStruct((N_EXP, T), F32),
                   jax.ShapeDtypeStruct((N_TILES, N_EXP, LANES), I32)],
        compiler_params=_cparams(("arbitrary",)),
        name="merge",
    )(x, mods, *ws1, ag_c, ag_l, am_c, am_l, u, u, u, *wsc, *ws2)


def _moe_tables(cnt):
    n8 = (cnt + SUB - 1) // SUB * SUB
    loc = jnp.cumsum(n8, axis=1) - n8
    tot = jnp.sum(n8, axis=0)
    ntile = (tot + PIECE - SUB + ET - 1) // ET
    seg = ntile * ET + SEG_SLACK
    off = jnp.cumsum(seg) - seg
    start = off[None, :] + jnp.cumsum(n8, axis=0) - n8
    npc = (n8 + PIECE - 1) // PIECE
    pc = jnp.cumsum(npc, axis=1)
    p = jnp.arange(N_PIECE_MAX, dtype=I32)[None, :, None]
    owner = jnp.logical_and((pc - npc)[:, None, :] <= p, p < pc[:, None, :])

    def of_owner(a):
        return jnp.sum(jnp.where(owner, a[:, None, :], 0), axis=2)

    k = p[:, :, 0] - of_owner(pc - npc)
    live = jnp.any(owner, axis=2)
    src = of_owner(loc) + PIECE * k
    dst = of_owner(start) + PIECE * k
    loc_c = (pc - npc) * PIECE
    src_c = of_owner(loc_c) + PIECE * k

    tcum = jnp.cumsum(ntile)
    g = jnp.arange(N_ET_MAX, dtype=I32)
    tile_owner = jnp.logical_and((tcum - ntile)[None, :] <= g[:, None], g[:, None] < tcum[None, :])
    trow = jnp.sum(jnp.where(tile_owner, off // ET + g[:, None] - (tcum - ntile), 0), axis=1)
    gstart = jnp.concatenate([jnp.zeros((1,), I32), tcum.astype(I32)])

    def col_table(a):
        return jnp.broadcast_to(a.astype(F32)[:, :, None], (N_TILES, N_EXP, LANES))

    def row_table(a):
        a = jnp.concatenate([a.astype(F32), jnp.zeros((N_TILES, LANES - N_EXP), F32)], axis=1)
        return jnp.broadcast_to(a[:, None, :], (N_TILES, SUB, LANES))

    return dict(src=jnp.where(live, src, 0).reshape(-1).astype(I32),
                src_c=jnp.where(live, src_c, 0).reshape(-1).astype(I32),
                dst=jnp.where(live, dst, 0).reshape(-1).astype(I32),
                n_piece=pc[:, -1].astype(I32), flush=(off + tot).astype(I32),
                first=(off // ET).astype(I32), ntile=ntile.astype(I32),
                trow=trow.astype(I32), gstart=gstart,
                lo_row=row_table(loc), hi_row=row_table(loc + n8),
                lo_col=col_table(loc_c), hi_col=col_table(loc_c + n8))


def _rank_in_tile(gate_t):
    sel = gate_t > 0.0
    r = lax.broadcasted_iota(I32, (TM, TM), 0)
    c = lax.broadcasted_iota(I32, (TM, TM), 1)
    upper = (r < c).astype(BF16)
    rank = _dot(sel.astype(BF16), upper)
    return jnp.where(sel, rank, -1e4)


def _piece_loop(n, body):
    def step(p, carry):
        body(p)
        return carry

    lax.fori_loop(0, n, step, 0)


def _dispatch_kernel(src_ref, dst_ref, npc_ref, flush_ref, gt_ref, lo_ref, hi_ref, h2_ref, xs_ref,
                     buf_ref, zero_ref, sem_ref, zsem_ref):
    i = pl.program_id(0)
    slot = i & 1

    def piece_copy(tile, p, s):
        src = src_ref[tile * N_PIECE_MAX + p]
        dst = dst_ref[tile * N_PIECE_MAX + p]
        return pltpu.make_async_copy(buf_ref.at[s, pl.ds(pl.multiple_of(src, SUB), PIECE)],
                                     xs_ref.at[pl.ds(pl.multiple_of(dst, SUB), PIECE)], sem_ref.at[s])

    @pl.when(i == 0)
    def _():
        zero_ref[...] = jnp.zeros(zero_ref.shape, F32)
        copies = [pltpu.make_async_copy(
            zero_ref, xs_ref.at[pl.ds(pl.multiple_of(flush_ref[e], SUB), FLUSH)], zsem_ref.at[0])
            for e in range(N_EXP)]
        for cp in copies:
            cp.start()
        for cp in copies:
            cp.wait()

    gate_t = gt_ref[...]
    rk = _rank_in_tile(gate_t).astype(BF16)
    g_hi, g_mid, g_lo = _split3(gate_t)
    h2 = h2_ref[...]
    lo = lo_ref[0:1, 0:N_EXP]
    hi = hi_ref[0:1, 0:N_EXP]
    for c in range(SORT_CHUNKS):
        r = (lax.broadcasted_iota(I32, (TM, N_EXP), 0) + c * TM).astype(F32)
        own = jnp.logical_and(r >= lo, r < hi)
        base = jnp.sum(jnp.where(own, lo, 0.0), axis=1, keepdims=True)
        owned = jnp.sum(jnp.where(own, 1.0, 0.0), axis=1, keepdims=True) > 0.0
        in_run = jnp.where(owned, r[:, 0:1] - base, -1.0)
        expand = own.astype(BF16)
        hit = _dot(expand, rk) == in_run
        gexp = _dot(expand, g_hi) + _dot(expand, g_mid) + _dot(expand, g_lo)
        w = jnp.sum(jnp.where(hit, gexp, 0.0), axis=1, keepdims=True)
        buf_ref[slot, c * TM:(c + 1) * TM, 0:D] = _dot(hit.astype(BF16), h2)
        buf_ref[slot, c * TM:(c + 1) * TM, D:XS_W] = jnp.broadcast_to(w, (TM, LANES))

    @pl.when(i > 0)
    def _():
        _piece_loop(npc_ref[i - 1], lambda p: piece_copy(i - 1, p, 1 - slot).wait())

    _piece_loop(npc_ref[i], lambda p: piece_copy(i, p, slot).start())

    @pl.when(i == pl.num_programs(0) - 1)
    def _():
        _piece_loop(npc_ref[i], lambda p: piece_copy(i, p, slot).wait())


def _dispatch_call(tb, gate_t, h2):
    tab = pl.BlockSpec((None, SUB, LANES), lambda i, *_: (i, 0, 0))
    return pl.pallas_call(
        _dispatch_kernel,
        grid_spec=pltpu.PrefetchScalarGridSpec(
            num_scalar_prefetch=4,
            grid=(N_TILES,),
            in_specs=[pl.BlockSpec((N_EXP, TM), lambda i, *_: (0, i)), tab, tab,
                      pl.BlockSpec((TM, D), lambda i, *_: (i, 0))],
            out_specs=pl.BlockSpec(memory_space=pl.ANY),
            scratch_shapes=[pltpu.VMEM((2, SORT_ROWS, XS_W), F32), pltpu.VMEM((FLUSH, XS_W), F32),
                            pltpu.SemaphoreType.DMA((2,)), pltpu.SemaphoreType.DMA((1,))]),
        out_shape=jax.ShapeDtypeStruct((N_SLOTS, XS_W), F32),
        compiler_params=_cparams(("arbitrary",)),
        name="moe_dispatch",
    )(tb["src"], tb["dst"], tb["n_piece"], tb["flush"], gate_t, tb["lo_row"], tb["hi_row"], h2)


def _expert_kernel(trow_ref, gstart_ref, bgu_ref, bdn_ref, xs_ref, wgu_ref, wdn_ref, ys_ref,
                   wgu_f32_ref, wdn_f32_ref, wgu_bf_ref, wdn_bf_ref, x_buf_ref, y_buf_ref,
                   wsem_ref, xsem_ref, ysem_ref, *, layer):
    e = pl.program_id(0)
    slot = e & 1

    def weight_copies(expert, s):
        return (pltpu.make_async_copy(wgu_ref.at[layer, expert], wgu_f32_ref.at[s], wsem_ref.at[0, s]),
                pltpu.make_async_copy(wdn_ref.at[layer, expert], wdn_f32_ref.at[s], wsem_ref.at[1, s]))

    n_live = gstart_ref[N_EXP]

    def x_copy(g, s):
        row = pl.multiple_of(trow_ref[g] * ET, ET)
        return pltpu.make_async_copy(xs_ref.at[pl.ds(row, ET)], x_buf_ref.at[s], xsem_ref.at[s])

    def y_copy(g, s):
        row = pl.multiple_of(trow_ref[g] * ET, ET)
        return pltpu.make_async_copy(y_buf_ref.at[s], ys_ref.at[pl.ds(row, ET)], ysem_ref.at[s])

    @pl.when(e == 0)
    def _():
        for cp in weight_copies(0, 0):
            cp.start(priority=1)

        @pl.when(n_live > 0)
        def _():
            x_copy(0, 0).start()

    @pl.when(e + 1 < pl.num_programs(0))
    def _():
        for cp in weight_copies(e + 1, 1 - slot):
            cp.start(priority=1)

    for cp in weight_copies(e, slot):
        cp.wait()
    for r in range(0, D, 128):
        wgu_bf_ref[r:r + 128, :] = wgu_f32_ref[slot, r:r + 128, :].astype(BF16)
    for r in range(0, FF, 128):
        wdn_bf_ref[r:r + 128, :] = wdn_f32_ref[slot, r:r + 128, :].astype(BF16)
    bgu = bgu_ref[...]
    bdn = bdn_ref[...]

    def row_tile(g, carry):
        s = g & 1

        @pl.when(g + 1 < n_live)
        def _():
            x_copy(g + 1, 1 - s).start()

        x_copy(g, s).wait()

        @pl.when(g >= 2)
        def _():
            y_copy(g - 2, s).wait()

        x = x_buf_ref[s, :, 0:D].astype(BF16)
        w = x_buf_ref[s, :, D:D + 1]
        gu = _dot(x, wgu_bf_ref[...]) + bgu
        gl = jnp.minimum(gu[:, 0:FF], SWIGLU_LIMIT)
        up = jnp.clip(gu[:, FF:2 * FF], -SWIGLU_LIMIT, SWIGLU_LIMIT)
        act = (up + 1.0) * (gl * jax.nn.sigmoid(SWIGLU_ALPHA * gl))
        y_buf_ref[s] = w * (_dot(act.astype(BF16), wdn_bf_ref[...]) + bdn)
        y_copy(g, s).start()
        return carry

    lax.fori_loop(gstart_ref[e], gstart_ref[e + 1], row_tile, 0)

    @pl.when(e == pl.num_programs(0) - 1)
    def _():
        @pl.when(n_live >= 2)
        def _():
            y_copy(n_live - 2, n_live & 1).wait()

        @pl.when(n_live >= 1)
        def _():
            y_copy(n_live - 1, (n_live - 1) & 1).wait()


def _expert_call(l, tb, xs, w_gu, b_gu, w_dn, b_dn):
    return pl.pallas_call(
        functools.partial(_expert_kernel, layer=l),
        grid_spec=pltpu.PrefetchScalarGridSpec(
            num_scalar_prefetch=2,
            grid=(N_EXP,),
            in_specs=[
                pl.BlockSpec((None, None, 1, 2 * FF), lambda e, *_: (l, e, 0, 0)),
                pl.BlockSpec((None, None, 1, D), lambda e, *_: (l, e, 0, 0)),
                pl.BlockSpec(memory_space=pl.ANY),
                pl.BlockSpec(memory_space=pl.ANY),
                pl.BlockSpec(memory_space=pl.ANY),
            ],
            out_specs=pl.BlockSpec(memory_space=pl.ANY),
            scratch_shapes=[pltpu.VMEM((2, D, 2 * FF), F32), pltpu.VMEM((2, FF, D), F32),
                            pltpu.VMEM((D, 2 * FF), BF16), pltpu.VMEM((FF, D), BF16),
                            pltpu.VMEM((2, ET, XS_W), F32), pltpu.VMEM((2, ET, D), F32),
                            pltpu.SemaphoreType.DMA((2, 2)), pltpu.SemaphoreType.DMA((2,)),
                            pltpu.SemaphoreType.DMA((2,))]),
        out_shape=jax.ShapeDtypeStruct((N_SLOTS, D), F32),
        compiler_params=_cparams(("arbitrary",)),
        name="moe_experts",
    )(tb["trow"], tb["gstart"], b_gu.reshape(DEPTH, N_EXP, 1, 2 * FF), b_dn.reshape(DEPTH, N_EXP, 1, D),
      xs, w_gu, w_dn)


def _combine_kernel(src_ref, dst_ref, npc_ref, gt_ref, lo_ref, hi_ref, x1_ref, mod_ref, ys_ref, o_ref,
                    buf_ref, sem_ref):
    i = pl.program_id(0)
    slot = i & 1

    def piece_copy(tile, p, s):
        src = src_ref[tile * N_PIECE_MAX + p]
        dst = dst_ref[tile * N_PIECE_MAX + p]
        return pltpu.make_async_copy(ys_ref.at[pl.ds(pl.multiple_of(dst, SUB), PIECE)],
                                     buf_ref.at[s, pl.ds(pl.multiple_of(src, SUB), PIECE)], sem_ref.at[s])

    @pl.when(i == 0)
    def _():
        buf_ref[...] = jnp.zeros(buf_ref.shape, F32)
        _piece_loop(npc_ref[0], lambda p: piece_copy(0, p, 0).start())

    @pl.when(i + 1 < pl.num_programs(0))
    def _():
        _piece_loop(npc_ref[i + 1], lambda p: piece_copy(i + 1, p, 1 - slot).start())

    rank_t = _rank_in_tile(gt_ref[...])
    r = lax.broadcasted_iota(I32, (TM, TM), 0)
    c = lax.broadcasted_iota(I32, (TM, TM), 1)
    eye = (r == c).astype(BF16)
    rk = _dot_nt(eye, rank_t.astype(BF16)).astype(BF16)
    lo = lo_ref[:, 0:1]
    hi = hi_ref[:, 0:1]

    _piece_loop(npc_ref[i], lambda p: piece_copy(i, p, slot).wait())

    def chunk(cidx, acc):
        rr = (lax.broadcasted_iota(I32, (N_EXP, TM), 1) + cidx * TM).astype(F32)
        own = jnp.logical_and(rr >= lo, rr < hi)
        base = jnp.sum(jnp.where(own, lo, 0.0), axis=0, keepdims=True)
        owned = jnp.sum(jnp.where(own, 1.0, 0.0), axis=0, keepdims=True) > 0.0
        in_run = jnp.where(owned, rr[0:1, :] - base, -1.0)
        hit = _dot(rk, own.astype(BF16)) == in_run
        y = buf_ref[slot, cidx * TM:(cidx + 1) * TM, :].astype(BF16)
        return acc + _dot(hit.astype(BF16), y)

    acc = jnp.zeros((TM, D), F32)
    for cidx in range(UNSORT_ROWS // TM):
        acc = chunk(cidx, acc)
    o_ref[...] = x1_ref[...] + mod_ref[...][:, 5 * D:6 * D] * acc


def _combine_call(l, tb, gate_t, x1, mods, ys):
    tab = pl.BlockSpec((None, N_EXP, LANES), lambda i, *_: (i, 0, 0))
    return pl.pallas_call(
        _combine_kernel,
        grid_spec=pltpu.PrefetchScalarGridSpec(
            num_scalar_prefetch=3,
            grid=(N_TILES,),
            in_specs=[
                pl.BlockSpec((N_EXP, TM), lambda i, *_: (0, i)), tab, tab,
                pl.BlockSpec((TM, D), lambda i, *_: (i, 0)),
                pl.BlockSpec((None, 1, 6 * D), lambda i, *_: (l * 8 + _grp(i), 0, 0)),
                pl.BlockSpec(memory_space=pl.ANY),
            ],
            out_specs=pl.BlockSpec((TM, D), lambda i, *_: (i, 0)),
            scratch_shapes=[pltpu.VMEM((2, UNSORT_ROWS, D), F32), pltpu.SemaphoreType.DMA((2,))]),
        out_shape=jax.ShapeDtypeStruct((T, D), F32),
        compiler_params=_cparams(("arbitrary",)),
        name="moe_combine",
    )(tb["src_c"], tb["dst"], tb["n_piece"], gate_t, tb["lo_col"], tb["hi_col"], x1, mods, ys)


def _rope_tables():
    t = jnp.arange(S_LAT, dtype=I32)
    row = (t // GRID_W).astype(F32)
    col = (t % GRID_W).astype(F32)

    def table(h, reps):
        inv = ROPE_THETA ** (-jnp.arange(0, h, 2, dtype=F32) / h)
        ar = row[:, None] * inv[None, :]
        ac = col[:, None] * inv[None, :]
        cos = jnp.concatenate([jnp.cos(ar), jnp.cos(ar), jnp.cos(ac), jnp.cos(ac)], axis=-1)
        sin = jnp.concatenate([-jnp.sin(ar), jnp.sin(ar), -jnp.sin(ac), jnp.sin(ac)], axis=-1)
        cos = jnp.tile(cos, (1, reps))
        sin = jnp.tile(sin, (1, reps))
        ident_c = jnp.ones((TM, LANES), F32)
        ident_s = jnp.zeros((TM, LANES), F32)
        return jnp.concatenate([ident_c, cos], axis=0), jnp.concatenate([ident_s, sin], axis=0)

    c64, s64 = table(HD // 2, 2)
    c32, s32 = table(MLA_ROPE // 2, 4)
    return c64, s64, c32, s32


def _group_matrix(n_rows, row_group, n_cols, col_group):
    r = np.arange(n_rows)[:, None] // row_group
    c = np.arange(n_cols)[None, :] // col_group
    return jnp.asarray((r == c).astype(np.float32), dtype=BF16)


def kernel(x_prompt, x_sample, cache_gqa_k, cache_gqa_v, cache_mla_ckv, cache_mla_krope, c, c_ctx,
           w_ada, b_ada, norm1_g, w_in, gqa_qn_g, gqa_kn_g, w_o_gqa, mla_cq_g, w_uq, mla_ckv_g, w_ukv,
           mla_qn_g, mla_kn_g, mla_krn_g, w_o_mla, conv_dw, conv_dw_b, conv_ln_g, conv_ln_b, w_pw2,
           w_out, norm2_g, w_router, b_router, w_gu, b_gu, w_dn, b_dn):
    L = DEPTH
    c64, s64, c32, s32 = _rope_tables()
    consts = dict(c64=c64, s64=s64, c32=c32, s32=s32,
                  b64=_group_matrix(512, HD, 512, HD), brn=_group_matrix(256, MLA_ROPE, 512, HD),
                  bnr=_group_matrix(512, HD, 256, MLA_ROPE),
                  brr=_group_matrix(256, MLA_ROPE, 256, MLA_ROPE))

    pad = jnp.zeros((L, D, LANES - MLA_ROPE), F32)
    wa = jnp.concatenate([w_in[:, :, 0:1440], pad, w_in[:, :, 1440:2464]], axis=-1).astype(BF16)
    wg = w_in[:, :, 2464:].astype(BF16)
    uq = w_uq.reshape(L, MLA_Q_RANK, NH, MLA_QK)
    uq = jnp.concatenate([uq[..., :HD].reshape(L, MLA_Q_RANK, NH * HD),
                          uq[..., HD:].reshape(L, MLA_Q_RANK, NH * MLA_ROPE)], axis=-1).astype(BF16)
    ukv = w_ukv.reshape(L, MLA_KV_RANK, NH, 2 * HD)
    ukv = jnp.concatenate([ukv[..., :HD].reshape(L, MLA_KV_RANK, NH * HD),
                           ukv[..., HD:].reshape(L, MLA_KV_RANK, NH * HD)], axis=-1).astype(BF16)

    def row3(a):
        return a.reshape(L, 1, a.shape[-1])

    wr_t = jnp.swapaxes(w_router, 1, 2)
    wr_hi = wr_t.astype(BF16)
    lw = dict(
        norm1_g=row3(norm1_g), wa=wa, wg=wg, w_uq=uq, w_ukv=ukv,
        gqa_qn_g=row3(jnp.tile(gqa_qn_g, (1, NH))), gqa_kn_g=row3(jnp.tile(gqa_kn_g, (1, GQA_KV))),
        mla_cq_g=row3(mla_cq_g), mla_ckv_g=row3(mla_ckv_g),
        mla_krn_g=row3(jnp.concatenate([mla_krn_g, jnp.zeros((L, LANES - MLA_ROPE), F32)], axis=-1)),
        mla_qn_n=row3(jnp.tile(mla_qn_g[:, :HD], (1, NH))),
        mla_qn_r=row3(jnp.tile(mla_qn_g[:, HD:], (1, NH))),
        mla_kn_g=row3(jnp.tile(mla_kn_g, (1, NH))),
        conv_dw=conv_dw, conv_dw_b=row3(conv_dw_b), conv_ln_g=row3(conv_ln_g), conv_ln_b=row3(conv_ln_b),
        w_o_gqa=w_o_gqa.astype(BF16), w_o_mla=w_o_mla.astype(BF16), w_pw2=w_pw2.astype(BF16),
        w_out=w_out.astype(BF16), norm2_g=row3(norm2_g),
        wr_hi=wr_hi, wr_lo=(wr_t - wr_hi.astype(F32)).astype(BF16),
        b_router=b_router.reshape(L, N_EXP, 1),
    )

    cvec = jnp.concatenate([c_ctx[None, :], c, jnp.zeros((8 - 1 - N_LAT_B, D), F32)], axis=0)
    mods = _ada_call(cvec, w_ada, b_ada).reshape(L * 8, 1, 6 * D)

    ck = cache_gqa_k.reshape(N_LAT_B, L, PAST, GQA_KV * HD)
    cv_ = cache_gqa_v.reshape(N_LAT_B, L, PAST, GQA_KV * HD)
    kr_placed = jnp.pad(cache_mla_krope, ((0, 0), (0, 0), (0, 0), (HD, LANES - MLA_QK)))
    km_c, vm_c = _cache_kv_call(cache_mla_ckv, kr_placed, ukv, lw["mla_kn_g"], consts["b64"])

    x = jnp.concatenate([x_prompt.reshape(T_CTX, D), x_sample.reshape(T_LAT, D)], axis=0)
    new_k, new_v, new_ckv, new_kr = [], [], [], []
    for l in range(L):
        qg, kg, vg, qm, ckv, kr, km, vm, u = _inproj_call(l, x, mods, lw, consts)
        new_k.append(kg[:T_CTX].reshape(N_CTX_B, S_CTX, GQA_KV, HD))
        new_v.append(vg[:T_CTX].reshape(N_CTX_B, S_CTX, GQA_KV, HD))
        new_ckv.append(ckv[:T_CTX].reshape(N_CTX_B, S_CTX, MLA_KV_RANK))
        new_kr.append(kr[:T_CTX, :MLA_ROPE].reshape(N_CTX_B, S_CTX, MLA_ROPE))
        ag_c = _attn_call(l, False, False, qg, kg, vg, None, None)
        ag_l = _attn_call(l, True, False, qg, kg, vg, ck, cv_)
        am_c = _attn_call(l, False, True, qm, km, vm, None, None)
        am_l = _attn_call(l, True, True, qm, km, vm, km_c, vm_c)
        x1, h2, gate_t, cnt = _merge_call(l, x, mods, lw, ag_c, ag_l, am_c, am_l, u)
        tb = _moe_tables(cnt[:, :, 0])
        xs = _dispatch_call(tb, gate_t, h2)
        ys = _expert_call(l, tb, xs, w_gu, b_gu, w_dn, b_dn)
        x = _combine_call(l, tb, gate_t, x1, mods, ys)

    return (x[:T_CTX].reshape(N_CTX_B, S_CTX, D), x[T_CTX:].reshape(N_LAT_B, S_LAT, D),
            jnp.stack(new_k, axis=1), jnp.stack(new_v, axis=1),
            jnp.stack(new_ckv, axis=1), jnp.stack(new_kr, axis=1))
```

```python
import functools

import numpy as np
import jax
import jax.numpy as jnp
from jax import lax
from jax.experimental import pallas as pl
from jax.experimental.pallas import tpu as pltpu

F32 = jnp.float32
BF16 = jnp.bfloat16
I32 = jnp.int32

D = 1024
DEPTH = 4
N_CTX_B, S_CTX = 16, 256
N_LAT_B, S_LAT = 2, 2048
PAST = 256
GRID_W = 64
ROPE_THETA = 10000.0
EPS = 1e-6
HD = 64
NH = 8
GQA_KV = 2
GQA_GROUP = NH // GQA_KV
MLA_Q_RANK = 384
MLA_KV_RANK = 256
MLA_ROPE = 32
MLA_QK = HD + MLA_ROPE
CONV_CH = 512
CONV_K = 31
CONV_PAD = CONV_K // 2
N_EXP = 32
TOP_K = 4
FF = 1024
SWIGLU_LIMIT = 7.0
SWIGLU_ALPHA = 1.702

T_CTX = N_CTX_B * S_CTX
T_LAT = N_LAT_B * S_LAT
T = T_CTX + T_LAT

LANES = 128
TM = 256
N_TILES = T // TM
CTX_TILES = T_CTX // TM
LAT_TILES_PER_B = S_LAT // TM
CTX_SEQ_PER_STEP = 1
TILES_PER_STEP = 2
WA = 2560
GATES_W = 3 * D
SUB = 8
PIECE = 16
ET = 256
XS_W = D + LANES
SORT_ROWS = 1280
SORT_CHUNKS = SORT_ROWS // TM
assert TM * TOP_K + (SUB - 1) * N_EXP + PIECE - SUB <= SORT_ROWS
UNSORT_ROWS = 1536
assert TM * TOP_K + (PIECE - 1) * N_EXP <= UNSORT_ROWS
N_PIECE_MAX = (TM * TOP_K + (SUB - 1) * N_EXP) // PIECE + N_EXP
FLUSH = ET + PIECE
SEG_SLACK = 2 * ET
N_ET_MAX = (T * TOP_K + (SUB - 1) * N_TILES * N_EXP + N_EXP * (PIECE - SUB)) // ET + N_EXP
N_SLOTS = N_ET_MAX * ET + N_EXP * SEG_SLACK
VMEM_LIMIT = 56 * 1024 * 1024


def _dot(a, b):
    return jnp.dot(a, b, preferred_element_type=F32)


def _dot_nt(a, b):
    return lax.dot_general(a, b, (((1,), (1,)), ((), ())), preferred_element_type=F32)


def _split3(x):
    hi = x.astype(BF16)
    r = x - hi.astype(F32)
    mid = r.astype(BF16)
    lo = (r - mid.astype(F32)).astype(BF16)
    return hi, mid, lo


def _cparams(sem, vmem=VMEM_LIMIT):
    return pltpu.CompilerParams(dimension_semantics=sem, vmem_limit_bytes=vmem)


def _grp(i):
    return jnp.where(i < CTX_TILES, 0, 1 + (i - CTX_TILES) // LAT_TILES_PER_B)


def _pos_blk(i):
    return jnp.where(i < CTX_TILES, 0, 1 + (i - CTX_TILES) % LAT_TILES_PER_B)


def _ada_kernel(c_ref, w_ref, b_ref, o_ref):
    c = c_ref[...]
    s = c * jax.nn.sigmoid(c)
    w = w_ref[...]
    s_hi, s_lo, _ = _split3(s)
    w_hi, w_lo, _ = _split3(w)
    o_ref[...] = _dot(s_hi, w_hi) + _dot(s_hi, w_lo) + _dot(s_lo, w_hi) + b_ref[...]


def _ada_call(cvec, w_ada, b_ada):
    tn = 1536
    return pl.pallas_call(
        _ada_kernel,
        grid=(DEPTH, 6 * D // tn),
        in_specs=[
            pl.BlockSpec((8, D), lambda l, j: (0, 0)),
            pl.BlockSpec((None, D, tn), lambda l, j: (l, 0, j)),
            pl.BlockSpec((None, 1, tn), lambda l, j: (l, 0, j)),
        ],
        out_specs=pl.BlockSpec((None, 8, tn), lambda l, j: (l, 0, j)),
        out_shape=jax.ShapeDtypeStruct((DEPTH, 8, 6 * D), F32),
        compiler_params=_cparams(("arbitrary", "arbitrary")),
        name="ada_mod",
    )(cvec, w_ada, b_ada.reshape(DEPTH, 1, 6 * D))


def _norm_mod(x, g, scale, shift):
    ms = jnp.mean(x * x, axis=-1, keepdims=True)
    return (x * lax.rsqrt(ms + EPS) * g) * (1.0 + scale) + shift


def _group_rms(x, ss, n, g):
    return x * lax.rsqrt(ss * (1.0 / n) + EPS) * g


def _partner(x, dist):
    n = x.shape[-1]
    lane = lax.broadcasted_iota(I32, x.shape, 1)
    r1 = pltpu.roll(x, dist, 1)
    r2 = pltpu.roll(x, n - dist, 1)
    r1_is_minus = pltpu.roll(lane, dist, 1) == ((lane + (n - dist)) & (n - 1))
    plus = jnp.where(r1_is_minus, r2, r1)
    minus = jnp.where(r1_is_minus, r1, r2)
    return jnp.where((lane & (2 * dist - 1)) < dist, plus, minus)


def _rope(x, cos, sin_signed, dist):
    return x * cos + _partner(x, dist) * sin_signed


def _tile_lanes(x, reps):
    return jnp.concatenate([x] * reps, axis=1)


def _lanes_up(x, s):
    lane = lax.broadcasted_iota(I32, x.shape, 1)
    r1 = pltpu.roll(x, s, 1)
    r2 = pltpu.roll(x, LANES - s, 1)
    r1_moves_up = pltpu.roll(lane, s, 1) == ((lane + (LANES - s)) & (LANES - 1))
    return jnp.where(r1_moves_up, r1, r2)


def _mla_head_blocks(nope, rope_blocks):
    lane = lax.broadcasted_iota(I32, (nope.shape[0], LANES), 1)
    blocks = []
    for h in range(NH):
        pair = nope[:, LANES * (h // 2):LANES * (h // 2) + LANES]
        if h % 2 == 1:
            pair = pltpu.roll(pair, HD, 1)
        blocks.append(jnp.where(lane < HD, pair, jnp.where(lane < MLA_QK, rope_blocks[h], 0.0)))
    return jnp.concatenate(blocks, axis=1).astype(BF16)


def _inproj_kernel(x_ref, mod_ref, g1_ref, wa_ref, wuq_ref, wukv_ref,
                   qng_ref, kng_ref, cqg_ref, ckvg_ref, krg_ref, mqn_ref, mqr_ref, mkn_ref,
                   c64_ref, s64_ref, c32_ref, s32_ref,
                   b64_ref, brn_ref, bnr_ref, brr_ref,
                   qg_ref, kg_ref, vg_ref, qm_ref, ckv_ref, kr_ref, km_ref, vm_ref, u_ref):
    mod = mod_ref[...]
    h = _norm_mod(x_ref[...], g1_ref[...], mod[:, D:2 * D], mod[:, 0:D])
    p = _dot(h.astype(BF16), wa_ref[...])
    b64 = b64_ref[...]
    c64 = c64_ref[...]
    s64 = s64_ref[...]
    c32 = c32_ref[...]
    s32 = s32_ref[...]

    q = p[:, 0:512]
    q = _group_rms(q, _dot((q * q).astype(BF16), b64), HD, qng_ref[...])
    q = _rope(q, _tile_lanes(c64, 4), _tile_lanes(s64, 4), 16)
    qg_ref[...] = (q * (HD ** -0.5)).astype(BF16)
    k = p[:, 512:640]
    k = _group_rms(k, _dot((k * k).astype(BF16), b64[0:128, 0:128]), HD, kng_ref[...])
    kg_ref[...] = _rope(k, c64, s64, 16)
    vg_ref[...] = p[:, 640:768]

    cq = p[:, 768:1152]
    cq = cq * lax.rsqrt(jnp.mean(cq * cq, axis=-1, keepdims=True) + EPS) * cqg_ref[...]
    qm = _dot(cq.astype(BF16), wuq_ref[...])
    qmn = qm[:, 0:512]
    qmr = qm[:, 512:768]
    sqn = (qmn * qmn).astype(BF16)
    sqr = (qmr * qmr).astype(BF16)
    ss_n = _dot(sqn, b64) + _dot(sqr, brn_ref[...])
    ss_r = _dot(sqn, bnr_ref[...]) + _dot(sqr, brr_ref[...])
    scale = MLA_QK ** -0.5
    qmn = _group_rms(qmn, ss_n, MLA_QK, mqn_ref[...]) * scale
    qmr = _group_rms(qmr, ss_r, MLA_QK, mqr_ref[...])
    qmr = _rope(qmr, _tile_lanes(c32, 2), _tile_lanes(s32, 2), 8) * scale
    q_rope = []
    for hd in range(NH):
        blk = qmr[:, LANES * (hd // 4):LANES * (hd // 4) + LANES]
        up = (HD - MLA_ROPE * (hd % 4)) % LANES
        q_rope.append(_lanes_up(blk, up) if up else blk)
    qm_ref[...] = _mla_head_blocks(qmn, q_rope)

    kr = p[:, 1408:1536]
    kr = kr * lax.rsqrt(jnp.sum(kr * kr, axis=-1, keepdims=True) * (1.0 / MLA_ROPE) + EPS) * krg_ref[...]
    kr = _rope(kr, c32, s32, 8)
    kr_ref[...] = kr

    ckv = p[:, 1152:1408]
    ckv = ckv * lax.rsqrt(jnp.mean(ckv * ckv, axis=-1, keepdims=True) + EPS) * ckvg_ref[...]
    ckv_ref[...] = ckv
    kv = _dot(ckv.astype(BF16), wukv_ref[...])
    kn = kv[:, 0:512]
    kn = _group_rms(kn, _dot((kn * kn).astype(BF16), b64), HD, mkn_ref[...])
    km_ref[...] = _mla_head_blocks(kn, [pltpu.roll(kr, HD, 1)] * NH)
    vm_ref[...] = kv[:, 512:1024].astype(BF16)

    u_ref[...] = p[:, 1536:2048] * jax.nn.sigmoid(p[:, 2048:2560])


def _inproj_call(l, x, mods, lw, consts):
    def full(a):
        return pl.BlockSpec(a.shape, lambda i: (0,) * a.ndim)

    def layer(a):
        return pl.BlockSpec((None,) + a.shape[1:], lambda i: (l,) + (0,) * (a.ndim - 1))

    def rows(w):
        return pl.BlockSpec((TM, w), lambda i: (i, 0))

    def tab():
        return pl.BlockSpec((TM, LANES), lambda i: (_pos_blk(i), 0))

    layer_ws = [lw["norm1_g"], lw["wa"], lw["w_uq"], lw["w_ukv"], lw["gqa_qn_g"], lw["gqa_kn_g"],
                lw["mla_cq_g"], lw["mla_ckv_g"], lw["mla_krn_g"], lw["mla_qn_n"], lw["mla_qn_r"],
                lw["mla_kn_g"]]
    tabs = [consts["c64"], consts["s64"], consts["c32"], consts["s32"]]
    bms = [consts["b64"], consts["brn"], consts["bnr"], consts["brr"]]
    out_w = [(512, BF16), (128, F32), (128, F32), (NH * LANES, BF16), (256, F32), (128, F32),
             (NH * LANES, BF16), (512, BF16), (512, F32)]
    return pl.pallas_call(
        _inproj_kernel,
        grid=(N_TILES,),
        in_specs=[rows(D), pl.BlockSpec((None, 1, 6 * D), lambda i: (l * 8 + _grp(i), 0, 0))]
        + [layer(a) for a in layer_ws] + [tab() for _ in tabs] + [full(a) for a in bms],
        out_specs=[rows(w) for w, _ in out_w],
        out_shape=[jax.ShapeDtypeStruct((T, w), dt) for w, dt in out_w],
        compiler_params=_cparams(("arbitrary",)),
        name="inproj",
    )(x, mods, *layer_ws, *tabs, *bms)


def _cache_kv_kernel(ckv_ref, kr_ref, wukv_ref, mkn_ref, b64_ref, km_ref, vm_ref):
    kv = _dot(ckv_ref[...].astype(BF16), wukv_ref[...])
    kn = kv[:, 0:512]
    kn = _group_rms(kn, _dot((kn * kn).astype(BF16), b64_ref[...]), HD, mkn_ref[...])
    km_ref[...] = _mla_head_blocks(kn, [kr_ref[...]] * NH)
    vm_ref[...] = kv[:, 512:1024].astype(BF16)


def _cache_kv_call(cache_ckv, cache_kr_placed, w_ukv, mla_kn_g, b64):
    def out(w):
        return pl.BlockSpec((None, None, PAST, w), lambda b, l: (b, l, 0, 0))

    return pl.pallas_call(
        _cache_kv_kernel,
        grid=(N_LAT_B, DEPTH),
        in_specs=[
            out(MLA_KV_RANK), out(LANES),
            pl.BlockSpec((None, MLA_KV_RANK, 1024), lambda b, l: (l, 0, 0)),
            pl.BlockSpec((None, 1, 512), lambda b, l: (l, 0, 0)),
            pl.BlockSpec((512, 512), lambda b, l: (0, 0)),
        ],
        out_specs=[out(NH * LANES), out(512)],
        out_shape=[jax.ShapeDtypeStruct((N_LAT_B, DEPTH, PAST, NH * LANES), BF16),
                   jax.ShapeDtypeStruct((N_LAT_B, DEPTH, PAST, 512), BF16)],
        compiler_params=_cparams(("arbitrary", "arbitrary")),
        name="cache_kv",
    )(cache_ckv, cache_kr_placed, w_ukv, mla_kn_g, b64)


def _attn_kernel(*refs, mla, has_cache, n_seq):
    if has_cache:
        q_ref, ks_ref, vs_ref, kc_ref, vc_ref, o_ref = refs
    else:
        q_ref, ks_ref, vs_ref, o_ref = refs
    qk_w = LANES if mla else HD
    n_k = ks_ref.shape[0] // n_seq

    def v_with_ones(v_ref, rows, blk, half):
        v = v_ref[rows, LANES * blk:LANES * blk + LANES].astype(BF16)
        lane = lax.broadcasted_iota(I32, v.shape, 1)
        return jnp.where(lane >= HD if half == 1 else lane < HD, v, jnp.ones_like(v))

    halves = [[] for _ in range(n_seq)]
    for j in range(n_seq):
        qrows = slice(j * TM, (j + 1) * TM)
        krows = slice(j * n_k, (j + 1) * n_k)
        if not has_cache:
            outs = []
            for h in range(NH):
                kvh = h if mla else h // GQA_GROUP
                s = _dot_nt(q_ref[qrows, qk_w * h:qk_w * h + qk_w],
                            ks_ref[krows, qk_w * kvh:qk_w * kvh + qk_w].astype(BF16))
                p = jnp.exp(s - jnp.max(s, axis=-1, keepdims=True))
                den = jnp.sum(p, axis=-1, keepdims=True)
                o = _dot(p.astype(BF16), vs_ref[krows, HD * kvh:HD * kvh + HD].astype(BF16))
                outs.append((o / den).astype(BF16))
            o_ref[qrows, :] = jnp.concatenate(outs, axis=1)
            continue
        for h in range(NH):
            kvh = h if mla else h // GQA_GROUP
            blk, half = kvh // 2, kvh % 2
            sl = slice(qk_w * kvh, qk_w * kvh + qk_w)
            qh = q_ref[qrows, qk_w * h:qk_w * h + qk_w]
            s = _dot_nt(qh, ks_ref[krows, sl].astype(BF16))
            m = jnp.max(s, axis=-1, keepdims=True)
            if has_cache:
                sc = _dot_nt(qh, kc_ref[:, sl].astype(BF16))
                m = jnp.maximum(m, jnp.max(sc, axis=-1, keepdims=True))
            o = _dot(jnp.exp((s - m).astype(BF16)), v_with_ones(vs_ref, krows, blk, half))
            if has_cache:
                o = o + _dot(jnp.exp((sc - m).astype(BF16)), v_with_ones(vc_ref, slice(None), blk, half))
            den = o[:, 0:1] if half == 1 else o[:, HD:HD + 1]
            on = o / den
            if half != h % 2:
                on = pltpu.roll(on, HD, 1)
            halves[j].append(on)
    lane = lax.broadcasted_iota(I32, (TM, LANES), 1)
    for j in range(n_seq if has_cache else 0):
        o_ref[j * TM:(j + 1) * TM, :] = jnp.concatenate(
            [jnp.where(lane < HD, halves[j][2 * b], halves[j][2 * b + 1]).astype(BF16)
             for b in range(NH // 2)], axis=1)


def _attn_call(l, latent, mla, q, ks, vs, kc, vc):
    if latent:
        nb, nq, n_seq, n_self, q0, k0 = N_LAT_B, S_LAT // TM, 1, S_LAT, CTX_TILES, T_CTX // S_LAT
    else:
        n_seq = CTX_SEQ_PER_STEP
        nb, nq, n_self, q0, k0 = N_CTX_B // n_seq, 1, n_seq * S_CTX, 0, 0
    tq = n_seq * TM

    def qspec(w):
        return pl.BlockSpec((tq, w), lambda b, i: (q0 + b * nq + i, 0))

    def kspec(w):
        return pl.BlockSpec((n_self, w), lambda b, i: (k0 + b, 0))

    def cspec(w):
        return pl.BlockSpec((None, None, PAST, w), lambda b, i: (b, l, 0, 0))

    args = [q, ks, vs]
    specs = [qspec(q.shape[1]), kspec(ks.shape[1]), kspec(vs.shape[1])]
    if latent:
        args += [kc, vc]
        specs += [cspec(ks.shape[1]), cspec(vs.shape[1])]
    return pl.pallas_call(
        functools.partial(_attn_kernel, mla=mla, has_cache=latent, n_seq=n_seq),
        grid=(nb, nq),
        in_specs=specs,
        out_specs=pl.BlockSpec((tq, 512), lambda b, i: (b * nq + i, 0)),
        out_shape=jax.ShapeDtypeStruct((nb * nq * tq, 512), BF16),
        compiler_params=_cparams(("arbitrary", "arbitrary")),
        name=("mla" if mla else "gqa") + ("_lat" if latent else "_ctx"),
    )(*args)


def _conv_tile(tile, prev16, cur, next16, dw_ref, dwb_ref, lng_ref, lnb_ref, win_ref, rot_ref):
    in_seq = (tile - CTX_TILES) % LAT_TILES_PER_B
    has_prev = jnp.logical_and(tile >= CTX_TILES, in_seq != 0)
    has_next = jnp.logical_and(tile >= CTX_TILES, in_seq != LAT_TILES_PER_B - 1)
    win_ref[0:16, :] = jnp.where(has_prev, prev16, 0.0)
    win_ref[16:16 + TM, :] = cur
    win_ref[16 + TM:32 + TM, :] = jnp.where(has_next, next16, 0.0)
    rot_rows = rot_ref.shape[1]
    for sh in range(1, SUB):
        rot_ref[sh - 1] = win_ref[sh:sh + rot_rows, :]
    acc = jnp.zeros((TM, CONV_CH), F32) + dwb_ref[...]
    for j in range(CONV_K):
        al, sh = (j + 1) // SUB * SUB, (j + 1) % SUB
        tap = win_ref[al:al + TM, :] if sh == 0 else rot_ref[sh - 1, al:al + TM, :]
        acc = acc + tap * dw_ref[j:j + 1, :]
    mu = jnp.mean(acc, axis=-1, keepdims=True)
    cen = acc - mu
    var = jnp.mean(cen * cen, axis=-1, keepdims=True)
    y = cen * lax.rsqrt(var + EPS) * lng_ref[...] + lnb_ref[...]
    return (y * jax.nn.sigmoid(y)).astype(BF16)


def _merge_kernel(x_ref, mod_ref, g1_ref, wg_ref, agc_ref, agl_ref, amc_ref, aml_ref,
                  up_ref, uc_ref, un_ref, dw_ref, dwb_ref, lng_ref, lnb_ref,
                  wog_ref, wom_ref, wpw_ref, wout_ref, g2_ref, wr_hi_ref, wr_lo_ref, br_ref,
                  x1_ref, h2_ref, gt_ref, cnt_ref, win_ref, rot_ref):
    step = pl.program_id(0)
    is_ctx = step < CTX_TILES // TILES_PER_STEP
    mod = mod_ref[...]
    last = TILES_PER_STEP - 1
    for j in range(TILES_PER_STEP):
        rows = slice(j * TM, (j + 1) * TM)
        prev16 = up_ref[TM - 16:TM, :] if j == 0 else uc_ref[j * TM - 16:j * TM, :]
        next16 = un_ref[0:16, :] if j == last else uc_ref[(j + 1) * TM:(j + 1) * TM + 16, :]
        cv = _conv_tile(step * TILES_PER_STEP + j, prev16, uc_ref[rows, :], next16,
                        dw_ref, dwb_ref, lng_ref, lnb_ref, win_ref.at[j], rot_ref.at[j])
        _merge_tile(rows, j, is_ctx, mod, cv, x_ref, g1_ref, wg_ref, agc_ref, agl_ref,
                    amc_ref, aml_ref, wog_ref, wom_ref, wpw_ref, wout_ref, g2_ref, wr_hi_ref,
                    wr_lo_ref, br_ref, x1_ref, h2_ref, gt_ref, cnt_ref)


def _merge_tile(rows, j, is_ctx, mod, cv, x_ref, g1_ref, wg_ref, agc_ref, agl_ref, amc_ref, aml_ref,
                wog_ref, wom_ref, wpw_ref, wout_ref, g2_ref, wr_hi_ref, wr_lo_ref, br_ref,
                x1_ref, h2_ref, gt_ref, cnt_ref):
    x = x_ref[rows, :]
    h = _norm_mod(x, g1_ref[...], mod[:, D:2 * D], mod[:, 0:D]).astype(BF16)
    gates = jax.nn.sigmoid(_dot(h, wg_ref[...]))
    ag = jnp.where(is_ctx, agc_ref[rows, :], agl_ref[rows, :])
    am = jnp.where(is_ctx, amc_ref[rows, :], aml_ref[rows, :])
    merged = (gates[:, 0:D] * _dot(ag, wog_ref[...])
              + gates[:, D:2 * D] * _dot(am, wom_ref[...])
              + gates[:, 2 * D:3 * D] * _dot(cv, wpw_ref[...]))
    x1 = x + mod[:, 2 * D:3 * D] * _dot(merged.astype(BF16), wout_ref[...])
    x1_ref[rows, :] = x1
    h2 = _norm_mod(x1, g2_ref[...], mod[:, 4 * D:5 * D], mod[:, 3 * D:4 * D])
    h2_hi = h2.astype(BF16)
    h2_ref[rows, :] = h2_hi
    h2_lo = (h2 - h2_hi.astype(F32)).astype(BF16)
    wr_hi = wr_hi_ref[...]
    logit = _dot_nt(wr_hi, h2_hi) + _dot_nt(wr_hi, h2_lo) + _dot_nt(wr_lo_ref[...], h2_hi) + br_ref[...]
    row = lax.broadcasted_iota(I32, logit.shape, 0).astype(F32)
    sels, vals = [], []
    for _ in range(TOP_K):
        m = jnp.max(logit, axis=0, keepdims=True)
        idx = jnp.min(jnp.where(logit == m, row, float(N_EXP)), axis=0, keepdims=True)
        sel = row == idx
        sels.append(sel)
        vals.append(m)
        logit = jnp.where(sel, -jnp.inf, logit)
    es = [jnp.exp(v - vals[0]) for v in vals]
    den = es[0] + es[1] + es[2] + es[3]
    gate = jnp.zeros(logit.shape, F32)
    for sel, e in zip(sels, es):
        gate = jnp.where(sel, e / den, gate)
    gt_ref[:, rows] = gate
    n = jnp.sum((gate > 0.0).astype(F32), axis=1, keepdims=True)
    cnt_ref[j] = jnp.broadcast_to(n, (N_EXP, LANES)).astype(I32)


def _merge_call(l, x, mods, lw, ag_c, ag_l, am_c, am_l, u):
    tps = TILES_PER_STEP
    tr = tps * TM

    def layer(a):
        return pl.BlockSpec((None,) + a.shape[1:], lambda i: (l,) + (0,) * (a.ndim - 1))

    def rows(w):
        return pl.BlockSpec((tr, w), lambda i: (i, 0))

    ctx_rows = pl.BlockSpec((tr, 512), lambda i: (jnp.minimum(i, CTX_TILES // tps - 1), 0))
    lat_rows = pl.BlockSpec((tr, 512), lambda i: (jnp.maximum(i - CTX_TILES // tps, 0), 0))
    u_prev = pl.BlockSpec((TM, CONV_CH), lambda i: (jnp.maximum(i * tps - 1, 0), 0))
    u_next = pl.BlockSpec((TM, CONV_CH), lambda i: (jnp.minimum((i + 1) * tps, N_TILES - 1), 0))
    ws1 = [lw["norm1_g"], lw["wg"]]
    wsc = [lw["conv_dw"], lw["conv_dw_b"], lw["conv_ln_g"], lw["conv_ln_b"]]
    ws2 = [lw["w_o_gqa"], lw["w_o_mla"], lw["w_pw2"], lw["w_out"], lw["norm2_g"],
           lw["wr_hi"], lw["wr_lo"], lw["b_router"]]
    return pl.pallas_call(
        _merge_kernel,
        grid=(N_TILES // tps,),
        in_specs=[rows(D), pl.BlockSpec((None, 1, 6 * D), lambda i: (l * 8 + _grp(i * tps), 0, 0))]
        + [layer(a) for a in ws1] + [ctx_rows, lat_rows, ctx_rows, lat_rows, u_prev, rows(CONV_CH), u_next]
        + [layer(a) for a in wsc] + [layer(a) for a in ws2],
        scratch_shapes=[pltpu.VMEM((tps, TM + 32, CONV_CH), F32),
                        pltpu.VMEM((tps, SUB - 1, TM + 24, CONV_CH), F32)],
        out_specs=[rows(D), rows(D), pl.BlockSpec((N_EXP, tr), lambda i: (0, i)),
                   pl.BlockSpec((tps, N_EXP, LANES), lambda i: (i, 0, 0))],
        out_shape=[jax.ShapeDtypeStruct((T, D), F32), jax.ShapeDtypeStruct((T, D), BF16),
                   jax.ShapeDtypeStruct((N_EXP, T), F32),
                   jax.ShapeDtypeStruct((N_TILES, N_EXP, LANES), I32)],
        compiler_params=_cparams(("arbitrary",)),
        name="merge",
    )(x, mods, *ws1, ag_c, ag_l, am_c, am_l, u, u, u, *wsc, *ws2)


def _moe_tables(cnt):
    n8 = (cnt + SUB - 1) // SUB * SUB
    loc = jnp.cumsum(n8, axis=1) - n8
    tot = jnp.sum(n8, axis=0)
    ntile = (tot + PIECE - SUB + ET - 1) // ET
    seg = ntile * ET + SEG_SLACK
    off = jnp.cumsum(seg) - seg
    start = off[None, :] + jnp.cumsum(n8, axis=0) - n8
    npc = (n8 + PIECE - 1) // PIECE
    pc = jnp.cumsum(npc, axis=1)
    p = jnp.arange(N_PIECE_MAX, dtype=I32)[None, :, None]
    owner = jnp.logical_and((pc - npc)[:, None, :] <= p, p < pc[:, None, :])

    def of_owner(a):
        return jnp.sum(jnp.where(owner, a[:, None, :], 0), axis=2)

    k = p[:, :, 0] - of_owner(pc - npc)
    live = jnp.any(owner, axis=2)
    src = of_owner(loc) + PIECE * k
    dst = of_owner(start) + PIECE * k
    loc_c = (pc - npc) * PIECE
    src_c = of_owner(loc_c) + PIECE * k

    tcum = jnp.cumsum(ntile)
    g = jnp.arange(N_ET_MAX, dtype=I32)
    tile_owner = jnp.logical_and((tcum - ntile)[None, :] <= g[:, None], g[:, None] < tcum[None, :])
    trow = jnp.sum(jnp.where(tile_owner, off // ET + g[:, None] - (tcum - ntile), 0), axis=1)
    gstart = jnp.concatenate([jnp.zeros((1,), I32), tcum.astype(I32)])

    def col_table(a):
        return jnp.broadcast_to(a.astype(F32)[:, :, None], (N_TILES, N_EXP, LANES))

    def row_table(a):
        a = jnp.concatenate([a.astype(F32), jnp.zeros((N_TILES, LANES - N_EXP), F32)], axis=1)
        return jnp.broadcast_to(a[:, None, :], (N_TILES, SUB, LANES))

    return dict(src=jnp.where(live, src, 0).reshape(-1).astype(I32),
                src_c=jnp.where(live, src_c, 0).reshape(-1).astype(I32),
                dst=jnp.where(live, dst, 0).reshape(-1).astype(I32),
                n_piece=pc[:, -1].astype(I32), flush=(off + tot).astype(I32),
                first=(off // ET).astype(I32), ntile=ntile.astype(I32),
                trow=trow.astype(I32), gstart=gstart,
                lo_row=row_table(loc), hi_row=row_table(loc + n8),
                lo_col=col_table(loc_c), hi_col=col_table(loc_c + n8))


def _rank_in_tile(gate_t):
    sel = gate_t > 0.0
    r = lax.broadcasted_iota(I32, (TM, TM), 0)
    c = lax.broadcasted_iota(I32, (TM, TM), 1)
    upper = (r < c).astype(BF16)
    rank = _dot(sel.astype(BF16), upper)
    return jnp.where(sel, rank, -1e4)


def _piece_loop(n, body):
    def step(p, carry):
        body(p)
        return carry

    lax.fori_loop(0, n, step, 0)


def _dispatch_kernel(src_ref, dst_ref, npc_ref, flush_ref, gt_ref, lo_ref, hi_ref, h2_ref, xs_ref,
                     buf_ref, zero_ref, sem_ref, zsem_ref):
    i = pl.program_id(0)
    slot = i & 1

    def piece_copy(tile, p, s):
        src = src_ref[tile * N_PIECE_MAX + p]
        dst = dst_ref[tile * N_PIECE_MAX + p]
        return pltpu.make_async_copy(buf_ref.at[s, pl.ds(pl.multiple_of(src, SUB), PIECE)],
                                     xs_ref.at[pl.ds(pl.multiple_of(dst, SUB), PIECE)], sem_ref.at[s])

    @pl.when(i == 0)
    def _():
        zero_ref[...] = jnp.zeros(zero_ref.shape, F32)
        copies = [pltpu.make_async_copy(
            zero_ref, xs_ref.at[pl.ds(pl.multiple_of(flush_ref[e], SUB), FLUSH)], zsem_ref.at[0])
            for e in range(N_EXP)]
        for cp in copies:
            cp.start()
        for cp in copies:
            cp.wait()

    gate_t = gt_ref[...]
    rk = _rank_in_tile(gate_t).astype(BF16)
    g_hi, g_mid, g_lo = _split3(gate_t)
    h2 = h2_ref[...]
    lo = lo_ref[0:1, 0:N_EXP]
    hi = hi_ref[0:1, 0:N_EXP]
    for c in range(SORT_CHUNKS):
        r = (lax.broadcasted_iota(I32, (TM, N_EXP), 0) + c * TM).astype(F32)
        own = jnp.logical_and(r >= lo, r < hi)
        base = jnp.sum(jnp.where(own, lo, 0.0), axis=1, keepdims=True)
        owned = jnp.sum(jnp.where(own, 1.0, 0.0), axis=1, keepdims=True) > 0.0
        in_run = jnp.where(owned, r[:, 0:1] - base, -1.0)
        expand = own.astype(BF16)
        hit = _dot(expand, rk) == in_run
        gexp = _dot(expand, g_hi) + _dot(expand, g_mid) + _dot(expand, g_lo)
        w = jnp.sum(jnp.where(hit, gexp, 0.0), axis=1, keepdims=True)
        buf_ref[slot, c * TM:(c + 1) * TM, 0:D] = _dot(hit.astype(BF16), h2)
        buf_ref[slot, c * TM:(c + 1) * TM, D:XS_W] = jnp.broadcast_to(w, (TM, LANES))

    @pl.when(i > 0)
    def _():
        _piece_loop(npc_ref[i - 1], lambda p: piece_copy(i - 1, p, 1 - slot).wait())

    _piece_loop(npc_ref[i], lambda p: piece_copy(i, p, slot).start())

    @pl.when(i == pl.num_programs(0) - 1)
    def _():
        _piece_loop(npc_ref[i], lambda p: piece_copy(i, p, slot).wait())


def _dispatch_call(tb, gate_t, h2):
    tab = pl.BlockSpec((None, SUB, LANES), lambda i, *_: (i, 0, 0))
    return pl.pallas_call(
        _dispatch_kernel,
        grid_spec=pltpu.PrefetchScalarGridSpec(
            num_scalar_prefetch=4,
            grid=(N_TILES,),
            in_specs=[pl.BlockSpec((N_EXP, TM), lambda i, *_: (0, i)), tab, tab,
                      pl.BlockSpec((TM, D), lambda i, *_: (i, 0))],
            out_specs=pl.BlockSpec(memory_space=pl.ANY),
            scratch_shapes=[pltpu.VMEM((2, SORT_ROWS, XS_W), F32), pltpu.VMEM((FLUSH, XS_W), F32),
                            pltpu.SemaphoreType.DMA((2,)), pltpu.SemaphoreType.DMA((1,))]),
        out_shape=jax.ShapeDtypeStruct((N_SLOTS, XS_W), F32),
        compiler_params=_cparams(("arbitrary",)),
        name="moe_dispatch",
    )(tb["src"], tb["dst"], tb["n_piece"], tb["flush"], gate_t, tb["lo_row"], tb["hi_row"], h2)


def _expert_kernel(trow_ref, gstart_ref, bgu_ref, bdn_ref, xs_ref, wgu_ref, wdn_ref, ys_ref,
                   wgu_f32_ref, wdn_f32_ref, wgu_bf_ref, wdn_bf_ref, x_buf_ref, y_buf_ref,
                   wsem_ref, xsem_ref, ysem_ref, *, layer):
    e = pl.program_id(0)
    slot = e & 1

    def weight_copies(expert, s):
        return (pltpu.make_async_copy(wgu_ref.at[layer, expert], wgu_f32_ref.at[s], wsem_ref.at[0, s]),
                pltpu.make_async_copy(wdn_ref.at[layer, expert], wdn_f32_ref.at[s], wsem_ref.at[1, s]))

    n_live = gstart_ref[N_EXP]

    def x_copy(g, s):
        row = pl.multiple_of(trow_ref[g] * ET, ET)
        return pltpu.make_async_copy(xs_ref.at[pl.ds(row, ET)], x_buf_ref.at[s], xsem_ref.at[s])

    def y_copy(g, s):
        row = pl.multiple_of(trow_ref[g] * ET, ET)
        return pltpu.make_async_copy(y_buf_ref.at[s], ys_ref.at[pl.ds(row, ET)], ysem_ref.at[s])

    @pl.when(e == 0)
    def _():
        for cp in weight_copies(0, 0):
            cp.start(priority=1)

        @pl.when(n_live > 0)
        def _():
            x_copy(0, 0).start()

    @pl.when(e + 1 < pl.num_programs(0))
    def _():
        for cp in weight_copies(e + 1, 1 - slot):
            cp.start(priority=1)

    for cp in weight_copies(e, slot):
        cp.wait()
    for r in range(0, D, 128):
        wgu_bf_ref[r:r + 128, :] = wgu_f32_ref[slot, r:r + 128, :].astype(BF16)
    for r in range(0, FF, 128):
        wdn_bf_ref[r:r + 128, :] = wdn_f32_ref[slot, r:r + 128, :].astype(BF16)
    bgu = bgu_ref[...]
    bdn = bdn_ref[...]

    def row_tile(g, carry):
        s = g & 1

        @pl.when(g + 1 < n_live)
        def _():
            x_copy(g + 1, 1 - s).start()

        x_copy(g, s).wait()

        @pl.when(g >= 2)
        def _():
            y_copy(g - 2, s).wait()

        x = x_buf_ref[s, :, 0:D].astype(BF16)
        w = x_buf_ref[s, :, D:D + 1]
        gu = _dot(x, wgu_bf_ref[...]) + bgu
        gl = jnp.minimum(gu[:, 0:FF], SWIGLU_LIMIT)
        up = jnp.clip(gu[:, FF:2 * FF], -SWIGLU_LIMIT, SWIGLU_LIMIT)
        act = (up + 1.0) * (gl * jax.nn.sigmoid(SWIGLU_ALPHA * gl))
        y_buf_ref[s] = w * (_dot(act.astype(BF16), wdn_bf_ref[...]) + bdn)
        y_copy(g, s).start()
        return carry

    lax.fori_loop(gstart_ref[e], gstart_ref[e + 1], row_tile, 0)

    @pl.when(e == pl.num_programs(0) - 1)
    def _():
        @pl.when(n_live >= 2)
        def _():
            y_copy(n_live - 2, n_live & 1).wait()

        @pl.when(n_live >= 1)
        def _():
            y_copy(n_live - 1, (n_live - 1) & 1).wait()


def _expert_call(l, tb, xs, w_gu, b_gu, w_dn, b_dn):
    return pl.pallas_call(
        functools.partial(_expert_kernel, layer=l),
        grid_spec=pltpu.PrefetchScalarGridSpec(
            num_scalar_prefetch=2,
            grid=(N_EXP,),
            in_specs=[
                pl.BlockSpec((None, None, 1, 2 * FF), lambda e, *_: (l, e, 0, 0)),
                pl.BlockSpec((None, None, 1, D), lambda e, *_: (l, e, 0, 0)),
                pl.BlockSpec(memory_space=pl.ANY),
                pl.BlockSpec(memory_space=pl.ANY),
                pl.BlockSpec(memory_space=pl.ANY),
            ],
            out_specs=pl.BlockSpec(memory_space=pl.ANY),
            scratch_shapes=[pltpu.VMEM((2, D, 2 * FF), F32), pltpu.VMEM((2, FF, D), F32),
                            pltpu.VMEM((D, 2 * FF), BF16), pltpu.VMEM((FF, D), BF16),
                            pltpu.VMEM((2, ET, XS_W), F32), pltpu.VMEM((2, ET, D), F32),
                            pltpu.SemaphoreType.DMA((2, 2)), pltpu.SemaphoreType.DMA((2,)),
                            pltpu.SemaphoreType.DMA((2,))]),
        out_shape=jax.ShapeDtypeStruct((N_SLOTS, D), F32),
        compiler_params=_cparams(("arbitrary",)),
        name="moe_experts",
    )(tb["trow"], tb["gstart"], b_gu.reshape(DEPTH, N_EXP, 1, 2 * FF), b_dn.reshape(DEPTH, N_EXP, 1, D),
      xs, w_gu, w_dn)


def _combine_kernel(src_ref, dst_ref, npc_ref, gt_ref, lo_ref, hi_ref, x1_ref, mod_ref, ys_ref, o_ref,
                    buf_ref, sem_ref):
    i = pl.program_id(0)
    slot = i & 1

    def piece_copy(tile, p, s):
        src = src_ref[tile * N_PIECE_MAX + p]
        dst = dst_ref[tile * N_PIECE_MAX + p]
        return pltpu.make_async_copy(ys_ref.at[pl.ds(pl.multiple_of(dst, SUB), PIECE)],
                                     buf_ref.at[s, pl.ds(pl.multiple_of(src, SUB), PIECE)], sem_ref.at[s])

    @pl.when(i == 0)
    def _():
        buf_ref[...] = jnp.zeros(buf_ref.shape, F32)
        _piece_loop(npc_ref[0], lambda p: piece_copy(0, p, 0).start())

    @pl.when(i + 1 < pl.num_programs(0))
    def _():
        _piece_loop(npc_ref[i + 1], lambda p: piece_copy(i + 1, p, 1 - slot).start())

    rank_t = _rank_in_tile(gt_ref[...])
    r = lax.broadcasted_iota(I32, (TM, TM), 0)
    c = lax.broadcasted_iota(I32, (TM, TM), 1)
    eye = (r == c).astype(BF16)
    rk = _dot_nt(eye, rank_t.astype(BF16)).astype(BF16)
    lo = lo_ref[:, 0:1]
    hi = hi_ref[:, 0:1]

    _piece_loop(npc_ref[i], lambda p: piece_copy(i, p, slot).wait())

    def chunk(cidx, acc):
        rr = (lax.broadcasted_iota(I32, (N_EXP, TM), 1) + cidx * TM).astype(F32)
        own = jnp.logical_and(rr >= lo, rr < hi)
        base = jnp.sum(jnp.where(own, lo, 0.0), axis=0, keepdims=True)
        owned = jnp.sum(jnp.where(own, 1.0, 0.0), axis=0, keepdims=True) > 0.0
        in_run = jnp.where(owned, rr[0:1, :] - base, -1.0)
        hit = _dot(rk, own.astype(BF16)) == in_run
        y = buf_ref[slot, cidx * TM:(cidx + 1) * TM, :].astype(BF16)
        return acc + _dot(hit.astype(BF16), y)

    acc = jnp.zeros((TM, D), F32)
    for cidx in range(UNSORT_ROWS // TM):
        acc = chunk(cidx, acc)
    o_ref[...] = x1_ref[...] + mod_ref[...][:, 5 * D:6 * D] * acc


def _combine_call(l, tb, gate_t, x1, mods, ys):
    tab = pl.BlockSpec((None, N_EXP, LANES), lambda i, *_: (i, 0, 0))
    return pl.pallas_call(
        _combine_kernel,
        grid_spec=pltpu.PrefetchScalarGridSpec(
            num_scalar_prefetch=3,
            grid=(N_TILES,),
            in_specs=[
                pl.BlockSpec((N_EXP, TM), lambda i, *_: (0, i)), tab, tab,
                pl.BlockSpec((TM, D), lambda i, *_: (i, 0)),
                pl.BlockSpec((None, 1, 6 * D), lambda i, *_: (l * 8 + _grp(i), 0, 0)),
                pl.BlockSpec(memory_space=pl.ANY),
            ],
            out_specs=pl.BlockSpec((TM, D), lambda i, *_: (i, 0)),
            scratch_shapes=[pltpu.VMEM((2, UNSORT_ROWS, D), F32), pltpu.SemaphoreType.DMA((2,))]),
        out_shape=jax.ShapeDtypeStruct((T, D), F32),
        compiler_params=_cparams(("arbitrary",)),
        name="moe_combine",
    )(tb["src_c"], tb["dst"], tb["n_piece"], gate_t, tb["lo_col"], tb["hi_col"], x1, mods, ys)


def _rope_tables():
    t = jnp.arange(S_LAT, dtype=I32)
    row = (t // GRID_W).astype(F32)
    col = (t % GRID_W).astype(F32)

    def table(h, reps):
        inv = ROPE_THETA ** (-jnp.arange(0, h, 2, dtype=F32) / h)
        ar = row[:, None] * inv[None, :]
        ac = col[:, None] * inv[None, :]
        cos = jnp.concatenate([jnp.cos(ar), jnp.cos(ar), jnp.cos(ac), jnp.cos(ac)], axis=-1)
        sin = jnp.concatenate([-jnp.sin(ar), jnp.sin(ar), -jnp.sin(ac), jnp.sin(ac)], axis=-1)
        cos = jnp.tile(cos, (1, reps))
        sin = jnp.tile(sin, (1, reps))
        ident_c = jnp.ones((TM, LANES), F32)
        ident_s = jnp.zeros((TM, LANES), F32)
        return jnp.concatenate([ident_c, cos], axis=0), jnp.concatenate([ident_s, sin], axis=0)

    c64, s64 = table(HD // 2, 2)
    c32, s32 = table(MLA_ROPE // 2, 4)
    return c64, s64, c32, s32


def _group_matrix(n_rows, row_group, n_cols, col_group):
    r = np.arange(n_rows)[:, None] // row_group
    c = np.arange(n_cols)[None, :] // col_group
    return jnp.asarray((r == c).astype(np.float32), dtype=BF16)


def kernel(x_prompt, x_sample, cache_gqa_k, cache_gqa_v, cache_mla_ckv, cache_mla_krope, c, c_ctx,
           w_ada, b_ada, norm1_g, w_in, gqa_qn_g, gqa_kn_g, w_o_gqa, mla_cq_g, w_uq, mla_ckv_g, w_ukv,
           mla_qn_g, mla_kn_g, mla_krn_g, w_o_mla, conv_dw, conv_dw_b, conv_ln_g, conv_ln_b, w_pw2,
           w_out, norm2_g, w_router, b_router, w_gu, b_gu, w_dn, b_dn):
    L = DEPTH
    c64, s64, c32, s32 = _rope_tables()
    consts = dict(c64=c64, s64=s64, c32=c32, s32=s32,
                  b64=_group_matrix(512, HD, 512, HD), brn=_group_matrix(256, MLA_ROPE, 512, HD),
                  bnr=_group_matrix(512, HD, 256, MLA_ROPE),
                  brr=_group_matrix(256, MLA_ROPE, 256, MLA_ROPE))

    pad = jnp.zeros((L, D, LANES - MLA_ROPE), F32)
    wa = jnp.concatenate([w_in[:, :, 0:1440], pad, w_in[:, :, 1440:2464]], axis=-1).astype(BF16)
    wg = w_in[:, :, 2464:].astype(BF16)
    uq = w_uq.reshape(L, MLA_Q_RANK, NH, MLA_QK)
    uq = jnp.concatenate([uq[..., :HD].reshape(L, MLA_Q_RANK, NH * HD),
                          uq[..., HD:].reshape(L, MLA_Q_RANK, NH * MLA_ROPE)], axis=-1).astype(BF16)
    ukv = w_ukv.reshape(L, MLA_KV_RANK, NH, 2 * HD)
    ukv = jnp.concatenate([ukv[..., :HD].reshape(L, MLA_KV_RANK, NH * HD),
                           ukv[..., HD:].reshape(L, MLA_KV_RANK, NH * HD)], axis=-1).astype(BF16)

    def row3(a):
        return a.reshape(L, 1, a.shape[-1])

    wr_t = jnp.swapaxes(w_router, 1, 2)
    wr_hi = wr_t.astype(BF16)
    lw = dict(
        norm1_g=row3(norm1_g), wa=wa, wg=wg, w_uq=uq, w_ukv=ukv,
        gqa_qn_g=row3(jnp.tile(gqa_qn_g, (1, NH))), gqa_kn_g=row3(jnp.tile(gqa_kn_g, (1, GQA_KV))),
        mla_cq_g=row3(mla_cq_g), mla_ckv_g=row3(mla_ckv_g),
        mla_krn_g=row3(jnp.concatenate([mla_krn_g, jnp.zeros((L, LANES - MLA_ROPE), F32)], axis=-1)),
        mla_qn_n=row3(jnp.tile(mla_qn_g[:, :HD], (1, NH))),
        mla_qn_r=row3(jnp.tile(mla_qn_g[:, HD:], (1, NH))),
        mla_kn_g=row3(jnp.tile(mla_kn_g, (1, NH))),
        conv_dw=conv_dw, conv_dw_b=row3(conv_dw_b), conv_ln_g=row3(conv_ln_g), conv_ln_b=row3(conv_ln_b),
        w_o_gqa=w_o_gqa.astype(BF16), w_o_mla=w_o_mla.astype(BF16), w_pw2=w_pw2.astype(BF16),
        w_out=w_out.astype(BF16), norm2_g=row3(norm2_g),
        wr_hi=wr_hi, wr_lo=(wr_t - wr_hi.astype(F32)).astype(BF16),
        b_router=b_router.reshape(L, N_EXP, 1),
    )

    cvec = jnp.concatenate([c_ctx[None, :], c, jnp.zeros((8 - 1 - N_LAT_B, D), F32)], axis=0)
    mods = _ada_call(cvec, w_ada, b_ada).reshape(L * 8, 1, 6 * D)

    ck = cache_gqa_k.reshape(N_LAT_B, L, PAST, GQA_KV * HD)
    cv_ = cache_gqa_v.reshape(N_LAT_B, L, PAST, GQA_KV * HD)
    kr_placed = jnp.pad(cache_mla_krope, ((0, 0), (0, 0), (0, 0), (HD, LANES - MLA_QK)))
    km_c, vm_c = _cache_kv_call(cache_mla_ckv, kr_placed, ukv, lw["mla_kn_g"], consts["b64"])

    x = jnp.concatenate([x_prompt.reshape(T_CTX, D), x_sample.reshape(T_LAT, D)], axis=0)
    new_k, new_v, new_ckv, new_kr = [], [], [], []
    for l in range(L):
        qg, kg, vg, qm, ckv, kr, km, vm, u = _inproj_call(l, x, mods, lw, consts)
        new_k.append(kg[:T_CTX].reshape(N_CTX_B, S_CTX, GQA_KV, HD))
        new_v.append(vg[:T_CTX].reshape(N_CTX_B, S_CTX, GQA_KV, HD))
        new_ckv.append(ckv[:T_CTX].reshape(N_CTX_B, S_CTX, MLA_KV_RANK))
        new_kr.append(kr[:T_CTX, :MLA_ROPE].reshape(N_CTX_B, S_CTX, MLA_ROPE))
        ag_c = _attn_call(l, False, False, qg, kg, vg, None, None)
        ag_l = _attn_call(l, True, False, qg, kg, vg, ck, cv_)
        am_c = _attn_call(l, False, True, qm, km, vm, None, None)
        am_l = _attn_call(l, True, True, qm, km, vm, km_c, vm_c)
        x1, h2, gate_t, cnt = _merge_call(l, x, mods, lw, ag_c, ag_l, am_c, am_l, u)
        tb = _moe_tables(cnt[:, :, 0])
        xs = _dispatch_call(tb, gate_t, h2)
        ys = _expert_call(l, tb, xs, w_gu, b_gu, w_dn, b_dn)
        x = _combine_call(l, tb, gate_t, x1, mods, ys)

    return (x[:T_CTX].reshape(N_CTX_B, S_CTX, D), x[T_CTX:].reshape(N_LAT_B, S_LAT, D),
            jnp.stack(new_k, axis=1), jnp.stack(new_v, axis=1),
            jnp.stack(new_ckv, axis=1), jnp.stack(new_kr, axis=1))
```
